```python
import math
import jax, jax.numpy as jnp
from jax import lax
import numpy as np

D_MODEL = 1024
BATCH = 4
SEQ = 8192
DEPTH = 4

N_MIXERS = 2
N_SELF_HEADS = 12
N_MEM_HEADS = 4
HEAD_DIM = 64
SELF_WIDTH = N_SELF_HEADS * HEAD_DIM
MEM_WIDTH = N_MEM_HEADS * HEAD_DIM
MIX_WIDTH = SELF_WIDTH + MEM_WIDTH
N_MEM = 256
D_FF = -(-8 * D_MODEL // (3 * 256)) * 256
FOX_Q_BLOCK = 128
MOBA_BLOCK = 256
MOBA_TOPK = 3
MOBA_Q_CHUNK = 32
N_FOX_LAYERS = (DEPTH + 1) // 2
N_MOBA_LAYERS = DEPTH // 2
FOX_PROJ = 3 * SELF_WIDTH + N_SELF_HEADS + MEM_WIDTH
MOBA_PROJ = 3 * SELF_WIDTH + MEM_WIDTH
NEG = -1e30
RMS_EPS = 1e-6

kernel_name = "hybrid_fox_moba_memory_trunk"


def rms_norm(x, g):
    xf = x.astype(jnp.float32)
    y = xf * lax.rsqrt(jnp.mean(xf * xf, axis=-1, keepdims=True) + RMS_EPS)
    return (y * g.astype(jnp.float32)).astype(x.dtype)


def split_heads(t, n_heads):
    b, s, _ = t.shape
    return t.reshape(b, s, n_heads, HEAD_DIM).transpose(0, 2, 1, 3)


def merge_heads(t):
    b, h, s, d = t.shape
    return t.transpose(0, 2, 1, 3).reshape(b, s, h * d)


def alibi_slopes(n_heads):
    return 2.0 ** (-8.0 * jnp.arange(1, n_heads + 1, dtype=jnp.float32) / n_heads)


def fox_attention(q, k, v, log_f):
    B, H, S, Dh = q.shape
    F = jnp.cumsum(log_f, axis=-1)
    scale = Dh ** -0.5
    kpos = jnp.arange(S)

    def one_block(i):
        start = i * FOX_Q_BLOCK
        qb = lax.dynamic_slice_in_dim(q, start, FOX_Q_BLOCK, axis=2)
        Fq = lax.dynamic_slice_in_dim(F, start, FOX_Q_BLOCK, axis=2)
        tq = start + jnp.arange(FOX_Q_BLOCK)
        s = (jnp.einsum('bhqd,bhkd->bhqk', qb, k).astype(jnp.float32) * scale
             + Fq[..., :, None] - F[..., None, :])
        s = jnp.where(tq[:, None] >= kpos[None, :], s, NEG)
        p = jax.nn.softmax(s, axis=-1)
        return jnp.einsum('bhqk,bhkd->bhqd', p.astype(v.dtype), v)

    out = lax.map(one_block, jnp.arange(S // FOX_Q_BLOCK))
    return out.transpose(1, 2, 0, 3, 4).reshape(B, H, S, Dh)


def moba_attention(q, k, v, slopes):
    B, H, S, Dh = q.shape
    L = MOBA_BLOCK
    nb = -(-S // L)
    S_pad = nb * L
    K = min(MOBA_TOPK, nb)
    pad = [(0, 0), (0, 0), (0, S_pad - S), (0, 0)]
    qp, kp, vp = jnp.pad(q, pad), jnp.pad(k, pad), jnp.pad(v, pad)
    kb = kp.reshape(B, H, nb, L, Dh)
    vb = vp.reshape(B, H, nb, L, Dh)
    kmean = jnp.mean(kb.astype(jnp.float32), axis=3)
    scale = Dh ** -0.5
    blk_ids = jnp.arange(nb)
    bi = jnp.arange(B)[:, None, None, None]
    hi = jnp.arange(H)[None, :, None, None]
    sl = slopes[None, :, None, None]

    def one_chunk(c):
        start = c * MOBA_Q_CHUNK
        qc = lax.dynamic_slice_in_dim(qp, start, MOBA_Q_CHUNK, axis=2)
        tq = start + jnp.arange(MOBA_Q_CHUNK)
        own = start // L
        gate = jnp.einsum('bhqd,bhnd->bhqn', qc.astype(jnp.float32), kmean)
        gate = jnp.where(blk_ids < own, gate, NEG)
        _, idx = lax.top_k(gate, K)
        valid = idx < own
        ksel = kb[bi, hi, idx]
        vsel = vb[bi, hi, idx]
        s_sel = jnp.einsum('bhqd,bhqkld->bhqkl', qc, ksel).astype(jnp.float32) * scale
        kpos_sel = idx[..., None] * L + jnp.arange(L)
        dist_sel = (tq[None, None, :, None, None] - kpos_sel).astype(jnp.float32)
        s_sel = jnp.where(valid[..., None], s_sel - sl[..., None] * dist_sel, NEG)
        s_sel = s_sel.reshape(B, H, MOBA_Q_CHUNK, K * L)
        kown = lax.dynamic_slice_in_dim(kp, own * L, L, axis=2)
        vown = lax.dynamic_slice_in_dim(vp, own * L, L, axis=2)
        opos = own * L + jnp.arange(L)
        dist_own = (tq[:, None] - opos[None, :]).astype(jnp.float32)
        s_own = (jnp.einsum('bhqd,bhld->bhql', qc, kown).astype(jnp.float32) * scale
                 - sl * dist_own)
        s_own = jnp.where(dist_own >= 0, s_own, NEG)
        p = jax.nn.softmax(jnp.concatenate([s_sel, s_own], axis=-1), axis=-1)
        p_sel = p[..., :K * L].astype(v.dtype)
        p_own = p[..., K * L:].astype(v.dtype)
        vsel = vsel.reshape(B, H, MOBA_Q_CHUNK, K * L, Dh)
        return (jnp.einsum('bhqm,bhqmd->bhqd', p_sel, vsel)
                + jnp.einsum('bhql,bhld->bhqd', p_own, vown))

    out = lax.map(one_chunk, jnp.arange(S_pad // MOBA_Q_CHUNK))
    out = out.transpose(1, 2, 0, 3, 4).reshape(B, H, S_pad, Dh)
    return out[:, :, :S]


def memory_attention(qm, mk, mv):
    s = jnp.einsum('bhsd,bhnd->bhsn', qm, mk).astype(jnp.float32) * (HEAD_DIM ** -0.5)
    p = jax.nn.softmax(s, axis=-1)
    return jnp.einsum('bhsn,bhnd->bhsd', p.astype(mv.dtype), mv)


def setup_inputs(seed: int = 0) -> dict:
    key = jax.random.key(seed)
    ks = jax.random.split(key, 14)
    f32 = jnp.float32
    x = jax.random.normal(ks[0], (BATCH, SEQ, D_MODEL), f32)
    mem = jax.random.normal(ks[1], (BATCH, N_MEM, D_MODEL), f32)
    norm_mix = 1.0 + 0.02 * jax.random.normal(ks[2], (DEPTH, D_MODEL), f32)
    norm_mem = 1.0 + 0.02 * jax.random.normal(ks[3], (DEPTH, D_MODEL), f32)
    norm_ffn = 1.0 + 0.02 * jax.random.normal(ks[4], (DEPTH, D_MODEL), f32)
    norm_final = 1.0 + 0.02 * jax.random.normal(ks[5], (D_MODEL,), f32)
    w_in_fox = jax.random.normal(ks[6], (N_FOX_LAYERS, D_MODEL, FOX_PROJ), f32) * D_MODEL ** -0.5
    b_fgate = jax.random.uniform(ks[7], (N_FOX_LAYERS, N_SELF_HEADS), f32, 1.0, 4.0)
    w_in_moba = jax.random.normal(ks[8], (N_MOBA_LAYERS, D_MODEL, MOBA_PROJ), f32) * D_MODEL ** -0.5
    w_mem_kv = jax.random.normal(ks[9], (DEPTH, D_MODEL, 2 * MEM_WIDTH), f32) * D_MODEL ** -0.5
    w_out = jax.random.normal(ks[10], (DEPTH, MIX_WIDTH, D_MODEL), f32) * MIX_WIDTH ** -0.5
    w_gate_up = jax.random.normal(ks[11], (DEPTH, D_MODEL, 2 * D_FF), f32) * D_MODEL ** -0.5
    w_down = jax.random.normal(ks[12], (DEPTH, D_FF, D_MODEL), f32) * D_FF ** -0.5
    return {"x": x, "mem": mem, "norm_mix": norm_mix, "norm_mem": norm_mem,
            "norm_ffn": norm_ffn, "norm_final": norm_final, "w_in_fox": w_in_fox,
            "b_fgate": b_fgate, "w_in_moba": w_in_moba, "w_mem_kv": w_mem_kv,
            "w_out": w_out, "w_gate_up": w_gate_up, "w_down": w_down}


def reference(x, mem, norm_mix, norm_mem, norm_ffn, norm_final, w_in_fox, b_fgate,
              w_in_moba, w_mem_kv, w_out, w_gate_up, w_down):
    slopes = alibi_slopes(N_SELF_HEADS)
    h = x
    for i in range(DEPTH):
        xn = rms_norm(h, norm_mix[i])
        mn = rms_norm(mem, norm_mem[i])
        mkv = mn @ w_mem_kv[i]
        mk = split_heads(mkv[..., :MEM_WIDTH], N_MEM_HEADS)
        mv = split_heads(mkv[..., MEM_WIDTH:], N_MEM_HEADS)
        j = i // N_MIXERS
        if i % N_MIXERS == 0:
            proj = xn @ w_in_fox[j]
            q = split_heads(proj[..., :SELF_WIDTH], N_SELF_HEADS)
            k = split_heads(proj[..., SELF_WIDTH:2 * SELF_WIDTH], N_SELF_HEADS)
            v = split_heads(proj[..., 2 * SELF_WIDTH:3 * SELF_WIDTH], N_SELF_HEADS)
            f_logit = proj[..., 3 * SELF_WIDTH:3 * SELF_WIDTH + N_SELF_HEADS]
            qm = split_heads(proj[..., 3 * SELF_WIDTH + N_SELF_HEADS:], N_MEM_HEADS)
            log_f = jax.nn.log_sigmoid(f_logit.astype(jnp.float32)
                                       + b_fgate[j].astype(jnp.float32))
            self_out = fox_attention(q, k, v, log_f.transpose(0, 2, 1))
        else:
            proj = xn @ w_in_moba[j]
            q = split_heads(proj[..., :SELF_WIDTH], N_SELF_HEADS)
            k = split_heads(proj[..., SELF_WIDTH:2 * SELF_WIDTH], N_SELF_HEADS)
            v = split_heads(proj[..., 2 * SELF_WIDTH:3 * SELF_WIDTH], N_SELF_HEADS)
            qm = split_heads(proj[..., 3 * SELF_WIDTH:], N_MEM_HEADS)
            self_out = moba_attention(q, k, v, slopes)
        mem_out = memory_attention(qm, mk, mv)
        heads = jnp.concatenate([merge_heads(self_out), merge_heads(mem_out)], axis=-1)
        h = h + heads @ w_out[i]
        hn = rms_norm(h, norm_ffn[i])
        gu = hn @ w_gate_up[i]
        h = h + (jax.nn.silu(gu[..., :D_FF]) * gu[..., D_FF:]) @ w_down[i]
    return rms_norm(h, norm_final)
```

```python
import functools

import jax
import jax.numpy as jnp
from jax import lax
from jax.experimental import pallas as pl
from jax.experimental.pallas import tpu as pltpu

D_MODEL = 1024
N_SELF_HEADS = 12
N_MEM_HEADS = 4
HEAD_DIM = 64
SELF_WIDTH = N_SELF_HEADS * HEAD_DIM
MEM_WIDTH = N_MEM_HEADS * HEAD_DIM
N_MEM = 256
D_FF = 2816
MOBA_BLOCK = 256
MOBA_TOPK = 3
RMS_EPS = 1e-6
NEG = -1e30
BELOW_NEG = -3e38

LANES = 128
N_PAIRS = N_SELF_HEADS // 2
AUG_WIDTH = N_SELF_HEADS * LANES
TOKEN_TILE = MOBA_BLOCK
FFN_TILE = 256
N_BIAS_LANES = 6
MAX_BLOCKS = 32
SCALE = HEAD_DIM ** -0.5
VMEM_LIMIT = 56 * 1024 * 1024

Q_OFF, K_OFF, V_OFF = 0, SELF_WIDTH, 2 * SELF_WIDTH
FOX_F_OFF = 3 * SELF_WIDTH
FOX_QM_OFF = FOX_F_OFF + LANES
FOX_PROJ_PAD = FOX_QM_OFF + MEM_WIDTH
MOBA_QM_OFF = 3 * SELF_WIDTH
MOBA_PROJ = MOBA_QM_OFF + MEM_WIDTH


def _nt_dot(a, b):
    return lax.dot_general(a, b, (((1,), (1,)), ((), ())),
                           preferred_element_type=jnp.float32)


def _dot(a, b):
    return jnp.dot(a, b, preferred_element_type=jnp.float32)


def _rms(xf, g):
    ms = jnp.mean(xf * xf, axis=-1, keepdims=True)
    return xf * lax.rsqrt(ms + RMS_EPS) * g


def _split3(x):
    hi = x.astype(jnp.bfloat16).astype(jnp.float32)
    r = x - hi
    mid = r.astype(jnp.bfloat16).astype(jnp.float32)
    return hi, mid, r - mid


def _bias_lanes(lane, base, q_terms, k_terms):
    zero = jnp.zeros(lane.shape, jnp.float32)
    ext_q = jnp.where((lane >= base + 3) & (lane < base + 6), 1.0, zero)
    ext_k = jnp.where((lane >= base) & (lane < base + 3), 1.0, zero)
    for c in range(3):
        ext_q = jnp.where(lane == base + c, q_terms[c], ext_q)
        ext_k = jnp.where(lane == base + 3 + c, k_terms[c], ext_k)
    return ext_q, ext_k


def _memory_attention(qm, mk_ref, mv_ref, hm_ref, lane):
    for p in range(N_MEM_HEADS // 2):
        blk = qm[:, LANES * p:LANES * (p + 1)]
        mk = mk_ref[:, LANES * p:LANES * (p + 1)]
        mv = mv_ref[:, LANES * p:LANES * (p + 1)]
        outs = []
        for e in range(2):
            in_head = (lane >= HEAD_DIM * e) & (lane < HEAD_DIM * (e + 1))
            qh = jnp.where(in_head, blk, 0.0).astype(jnp.bfloat16)
            s = _nt_dot(qh, mk) * SCALE
            s = s - jnp.max(s, axis=-1, keepdims=True)
            pr = jnp.exp(s)
            den = jnp.sum(pr, axis=-1, keepdims=True)
            outs.append(_dot(pr.astype(jnp.bfloat16), mv) / den)
        out = jnp.where(lane < HEAD_DIM, outs[0], outs[1])
        hm_ref[:, LANES * p:LANES * (p + 1)] = out.astype(hm_ref.dtype)


def _fox_proj_kernel(h_ref, g_ref, w_ref, b_ref, mk_ref, mv_ref,
                     qa_ref, ka_ref, v_ref, hm_ref, carry_ref):
    i = pl.program_id(1)
    tm = h_ref.shape[0]
    xn = _rms(h_ref[...], g_ref[...]).astype(jnp.bfloat16)
    lane = lax.broadcasted_iota(jnp.int32, (tm, LANES), 1)

    v_ref[...] = _dot(xn, w_ref[:, V_OFF:V_OFF + SELF_WIDTH]).astype(v_ref.dtype)
    qm = _dot(xn, w_ref[:, FOX_QM_OFF:FOX_QM_OFF + MEM_WIDTH])
    _memory_attention(qm, mk_ref, mv_ref, hm_ref, lane)

    f_logit = _dot(xn, w_ref[:, FOX_F_OFF:FOX_F_OFF + LANES]) + b_ref[...]
    log_f = jnp.minimum(f_logit, 0.0) - jnp.log(1.0 + jnp.exp(-jnp.abs(f_logit)))
    row = lax.broadcasted_iota(jnp.int32, (tm, tm), 0)
    col = lax.broadcasted_iota(jnp.int32, (tm, tm), 1)
    tri = jnp.where(row >= col, 1.0, 0.0).astype(jnp.bfloat16)
    hi, mid, lo = _split3(log_f)
    parts = jnp.concatenate([hi, mid, lo], axis=1).astype(jnp.bfloat16)
    cs = _dot(tri, parts)
    local = cs[:, :LANES] + cs[:, LANES:2 * LANES] + cs[:, 2 * LANES:]

    @pl.when(i == 0)
    def _():
        carry_ref[...] = jnp.zeros_like(carry_ref)

    f_cum = local + carry_ref[0:1, :]
    carry_ref[0:1, :] = f_cum[tm - 1:tm, :]
    f_hi, f_mid, f_lo = _split3(f_cum)

    q = _dot(xn, w_ref[:, Q_OFF:Q_OFF + SELF_WIDTH])
    k = _dot(xn, w_ref[:, K_OFF:K_OFF + SELF_WIDTH])
    for p in range(N_PAIRS):
        qp = q[:, LANES * p:LANES * (p + 1)] * SCALE
        kp = k[:, LANES * p:LANES * (p + 1)]
        for e in range(2):
            hd = 2 * p + e
            base = HEAD_DIM * (1 - e)
            in_head = (lane >= HEAD_DIM * e) & (lane < HEAD_DIM * (e + 1))
            cols = [t[:, hd:hd + 1] for t in (f_hi, f_mid, f_lo)]
            ext_q, ext_k = _bias_lanes(lane, base, cols, [-c for c in cols])
            sl = slice(LANES * hd, LANES * (hd + 1))
            qa_ref[:, sl] = jnp.where(in_head, qp, ext_q).astype(qa_ref.dtype)
            ka_ref[:, sl] = jnp.where(in_head, kp, ext_k).astype(ka_ref.dtype)


def _moba_proj_kernel(h_ref, g_ref, w_ref, mk_ref, mv_ref,
                      qa_ref, ka_ref, v_ref, hm_ref, km_ref, *, slopes):
    i = pl.program_id(1)
    tm = h_ref.shape[0]
    xn = _rms(h_ref[...], g_ref[...]).astype(jnp.bfloat16)
    lane = lax.broadcasted_iota(jnp.int32, (tm, LANES), 1)
    lane_row = lax.broadcasted_iota(jnp.int32, (1, LANES), 1)

    v_ref[...] = _dot(xn, w_ref[:, V_OFF:V_OFF + SELF_WIDTH]).astype(v_ref.dtype)
    qm = _dot(xn, w_ref[:, MOBA_QM_OFF:MOBA_QM_OFF + MEM_WIDTH])
    _memory_attention(qm, mk_ref, mv_ref, hm_ref, lane)

    @pl.when(i == 0)
    def _():
        km_ref[...] = jnp.zeros_like(km_ref)

    q = _dot(xn, w_ref[:, Q_OFF:Q_OFF + SELF_WIDTH])
    k = _dot(xn, w_ref[:, K_OFF:K_OFF + SELF_WIDTH])
    kmean = jnp.mean(k, axis=0, keepdims=True)
    pos = (i * tm + lax.broadcasted_iota(jnp.int32, (tm, 1), 0)).astype(jnp.float32)

    for p in range(N_PAIRS):
        qp = q[:, LANES * p:LANES * (p + 1)]
        kp = k[:, LANES * p:LANES * (p + 1)]
        q_hi = qp.astype(jnp.bfloat16)
        q_lo = (qp - q_hi.astype(jnp.float32)).astype(jnp.bfloat16)
        for e in range(2):
            hd = 2 * p + e
            base = HEAD_DIM * (1 - e)
            sel0 = base + N_BIAS_LANES
            in_head = (lane >= HEAD_DIM * e) & (lane < HEAD_DIM * (e + 1))

            km = km_ref[hd]
            km_hi = km.astype(jnp.bfloat16)
            km_lo = (km - km_hi.astype(jnp.float32)).astype(jnp.bfloat16)
            gate = _nt_dot(q_hi, km_hi) + _nt_dot(q_lo, km_hi) + _nt_dot(q_hi, km_lo)
            g = jnp.where((lane >= sel0) & (lane < sel0 + i), gate, BELOW_NEG)
            chosen = lane == sel0 + i
            for _ in range(MOBA_TOPK):
                best = jnp.max(g, axis=-1, keepdims=True)
                first = jnp.min(jnp.where(g == best, lane, 2 * LANES),
                                axis=-1, keepdims=True)
                hit = lane == first
                chosen = chosen | (hit & (best > 0.5 * BELOW_NEG))
                g = jnp.where(hit, BELOW_NEG, g)
            pen = jnp.where(chosen, 0.0, NEG)

            slope = slopes[hd]
            q_terms = _split3(-slope * pos)
            k_terms = _split3(slope * pos)
            ext_q, ext_k = _bias_lanes(lane, base, q_terms, k_terms)
            in_sel = (lane >= sel0) & (lane < sel0 + MAX_BLOCKS)
            ext_q = jnp.where(in_sel, pen, ext_q)
            ext_k = jnp.where(lane == sel0 + i, 1.0, ext_k)
            sl = slice(LANES * hd, LANES * (hd + 1))
            qa_ref[:, sl] = jnp.where(in_head, qp * SCALE, ext_q).astype(qa_ref.dtype)
            ka_ref[:, sl] = jnp.where(in_head, kp, ext_k).astype(ka_ref.dtype)

            in_head_row = (lane_row >= HEAD_DIM * e) & (lane_row < HEAD_DIM * (e + 1))
            km_row = jnp.where(in_head_row, kmean[:, LANES * p:LANES * (p + 1)], 0.0)
            km_ref[hd, pl.ds(sel0 + i, 1), :] = km_row


def _flash_kernel(qa_ref, ka_ref, v_ref, o_ref, m_ref, l_ref, acc_ref):
    i = pl.program_id(2)
    tq = qa_ref.shape[0]
    tk = tq
    lane = lax.broadcasted_iota(jnp.int32, (tq, LANES), 1)

    m_ref[...] = jnp.full(m_ref.shape, NEG, jnp.float32)
    l_ref[...] = jnp.zeros(l_ref.shape, jnp.float32)
    acc_ref[...] = jnp.zeros(acc_ref.shape, jnp.float32)

    def step(j, causal):
        rows = pl.ds(pl.multiple_of(j * tk, tk), tk)
        v = v_ref[rows, :]
        for e in range(2):
            q = qa_ref[:, LANES * e:LANES * (e + 1)]
            k = ka_ref[rows, LANES * e:LANES * (e + 1)]
            s = _nt_dot(q, k)
            if causal:
                r = lax.broadcasted_iota(jnp.int32, (tq, tk), 0)
                c = lax.broadcasted_iota(jnp.int32, (tq, tk), 1)
                s = jnp.where(r >= c, s, NEG)
            m_prev = m_ref[e]
            m_new = jnp.maximum(m_prev, jnp.max(s, axis=-1, keepdims=True))
            alpha = jnp.exp(m_prev - m_new)
            pr = jnp.exp(s - m_new)
            l_ref[e] = alpha * l_ref[e] + jnp.sum(pr, axis=-1, keepdims=True)
            acc_ref[e] = alpha * acc_ref[e] + _dot(pr.astype(jnp.bfloat16), v)
            m_ref[e] = m_new

    def body(j, carry):
        step(j, False)
        return carry

    lax.fori_loop(0, i, body, 0)
    step(i, True)

    o0 = acc_ref[0] / l_ref[0]
    o1 = acc_ref[1] / l_ref[1]
    o_ref[...] = jnp.where(lane < HEAD_DIM, o0, o1).astype(o_ref.dtype)


def _ffn_kernel(hs_ref, hm_ref, h_ref, woa_ref, wob_ref, g_ref, wgu_ref, wd_ref,
                gf_ref, o_ref, *, final):
    h1 = h_ref[...] + _dot(hs_ref[...], woa_ref[...]) + _dot(hm_ref[...], wob_ref[...])
    hn = _rms(h1, g_ref[...]).astype(jnp.bfloat16)
    ffn = None
    for c0, c1 in ((0, 1536), (1536, D_FF)):
        gate = _dot(hn, wgu_ref[:, c0:c1])
        up = _dot(hn, wgu_ref[:, D_FF + c0:D_FF + c1])
        act = (gate * jax.nn.sigmoid(gate) * up).astype(jnp.bfloat16)
        down = _dot(act, wd_ref[c0:c1, :])
        ffn = down if ffn is None else ffn + down
    y = h1 + ffn
    if final:
        y = _rms(y, gf_ref[...])
    o_ref[...] = y


def _mem_kv_kernel(mem_ref, g_ref, w_ref, o_ref):
    mn = _rms(mem_ref[...], g_ref[0]).astype(jnp.bfloat16)
    o_ref[0] = _dot(mn, w_ref[0]).astype(o_ref.dtype)


def _params(sem):
    return pltpu.CompilerParams(dimension_semantics=sem, vmem_limit_bytes=VMEM_LIMIT)


def _const_spec(shape):
    nd = len(shape)
    return pl.BlockSpec(shape, lambda *_: (0,) * nd)


def _mem_kv(mem2, norm_mem, w_mem_kv, batch):
    depth = w_mem_kv.shape[0]
    return pl.pallas_call(
        _mem_kv_kernel,
        grid=(depth, batch),
        in_specs=[pl.BlockSpec((N_MEM, D_MODEL), lambda d, b: (b, 0)),
                  pl.BlockSpec((1, 1, D_MODEL), lambda d, b: (d, 0, 0)),
                  pl.BlockSpec((1, D_MODEL, 2 * MEM_WIDTH), lambda d, b: (d, 0, 0))],
        out_specs=pl.BlockSpec((1, N_MEM, 2 * MEM_WIDTH), lambda d, b: (d, b, 0)),
        out_shape=jax.ShapeDtypeStruct((depth, batch * N_MEM, 2 * MEM_WIDTH), jnp.bfloat16),
        compiler_params=_params(("arbitrary", "arbitrary")),
        name="mem_kv",
    )(mem2, norm_mem.reshape(depth, 1, D_MODEL), w_mem_kv)


def _proj_call(kernel_fn, name, h, gain, w, extra, mkv, batch, seq, scratch):
    nt = seq // TOKEN_TILE
    tokens = batch * seq
    tok = lambda b, i: (b * nt + i, 0)
    in_specs = [pl.BlockSpec((TOKEN_TILE, D_MODEL), tok),
                _const_spec((1, D_MODEL)),
                _const_spec(w.shape)]
    args = [h, gain, w]
    for a in extra:
        in_specs.append(_const_spec(a.shape))
        args.append(a)
    in_specs += [pl.BlockSpec((N_MEM, MEM_WIDTH), lambda b, i: (b, 0)),
                 pl.BlockSpec((N_MEM, MEM_WIDTH), lambda b, i: (b, 1))]
    args += [mkv, mkv]
    out_specs = [pl.BlockSpec((TOKEN_TILE, AUG_WIDTH), tok),
                 pl.BlockSpec((TOKEN_TILE, AUG_WIDTH), tok),
                 pl.BlockSpec((TOKEN_TILE, SELF_WIDTH), tok),
                 pl.BlockSpec((TOKEN_TILE, MEM_WIDTH), tok)]
    out_shape = [jax.ShapeDtypeStruct((tokens, AUG_WIDTH), jnp.bfloat16),
                 jax.ShapeDtypeStruct((tokens, AUG_WIDTH), jnp.bfloat16),
                 jax.ShapeDtypeStruct((tokens, SELF_WIDTH), jnp.bfloat16),
                 jax.ShapeDtypeStruct((tokens, MEM_WIDTH), jnp.bfloat16)]
    return pl.pallas_call(
        kernel_fn,
        grid=(batch, nt),
        in_specs=in_specs,
        out_specs=out_specs,
        out_shape=out_shape,
        scratch_shapes=scratch,
        compiler_params=_params(("arbitrary", "arbitrary")),
        name=name,
    )(*args)


def _flash(qa, ka, v, batch, seq):
    nq = seq // TOKEN_TILE
    tokens = batch * seq
    return pl.pallas_call(
        _flash_kernel,
        grid=(batch, N_PAIRS, nq),
        in_specs=[pl.BlockSpec((TOKEN_TILE, 2 * LANES), lambda b, p, i: (b * nq + i, p)),
                  pl.BlockSpec((seq, 2 * LANES), lambda b, p, i: (b, p)),
                  pl.BlockSpec((seq, LANES), lambda b, p, i: (b, p))],
        out_specs=pl.BlockSpec((TOKEN_TILE, LANES), lambda b, p, i: (b * nq + i, p)),
        out_shape=jax.ShapeDtypeStruct((tokens, SELF_WIDTH), jnp.bfloat16),
        scratch_shapes=[pltpu.VMEM((2, TOKEN_TILE, 1), jnp.float32),
                        pltpu.VMEM((2, TOKEN_TILE, 1), jnp.float32),
                        pltpu.VMEM((2, TOKEN_TILE, LANES), jnp.float32)],
        compiler_params=_params(("arbitrary", "arbitrary", "arbitrary")),
        name="flash_attention",
    )(qa, ka, v)


def _ffn(hs, hm, h, woa, wob, g, wgu, wd, gf, final):
    tokens = h.shape[0]
    tok = lambda t: (t, 0)
    return pl.pallas_call(
        functools.partial(_ffn_kernel, final=final),
        grid=(tokens // FFN_TILE,),
        in_specs=[pl.BlockSpec((FFN_TILE, SELF_WIDTH), tok),
                  pl.BlockSpec((FFN_TILE, MEM_WIDTH), tok),
                  pl.BlockSpec((FFN_TILE, D_MODEL), tok),
                  _const_spec(woa.shape), _const_spec(wob.shape),
                  _const_spec((1, D_MODEL)),
                  _const_spec(wgu.shape), _const_spec(wd.shape),
                  _const_spec((1, D_MODEL))],
        out_specs=pl.BlockSpec((FFN_TILE, D_MODEL), tok),
        out_shape=jax.ShapeDtypeStruct((tokens, D_MODEL), jnp.float32),
        compiler_params=_params(("arbitrary",)),
        name="out_proj_ffn",
    )(hs, hm, h, woa, wob, g, wgu, wd, gf)


def kernel(x, mem, norm_mix, norm_mem, norm_ffn, norm_final, w_in_fox, b_fgate,
           w_in_moba, w_mem_kv, w_out, w_gate_up, w_down):
    batch, seq, _ = x.shape
    depth = w_out.shape[0]
    bf16 = jnp.bfloat16
    assert seq % TOKEN_TILE == 0 and seq // MOBA_BLOCK <= MAX_BLOCKS
    assert (batch * seq) % FFN_TILE == 0

    slopes = tuple(2.0 ** (-8.0 * (hd + 1) / N_SELF_HEADS) for hd in range(N_SELF_HEADS))
    h = x.reshape(batch * seq, D_MODEL)
    mkv_all = _mem_kv(mem.reshape(batch * N_MEM, D_MODEL), norm_mem,
                      w_mem_kv.astype(bf16), batch)
    gf = norm_final.reshape(1, D_MODEL)

    for i in range(depth):
        j = i // 2
        gain = norm_mix[i].reshape(1, D_MODEL)
        if i % 2 == 0:
            w = w_in_fox[j]
            f_cols = jnp.pad(w[:, FOX_F_OFF:FOX_F_OFF + N_SELF_HEADS],
                             ((0, 0), (0, LANES - N_SELF_HEADS)))
            w = jnp.concatenate([w[:, :FOX_F_OFF], f_cols,
                                 w[:, FOX_F_OFF + N_SELF_HEADS:]], axis=1).astype(bf16)
            bias = jnp.pad(b_fgate[j], (0, LANES - N_SELF_HEADS)).reshape(1, LANES)
            qa, ka, v, hm = _proj_call(
                _fox_proj_kernel, "fox_proj", h, gain, w, [bias], mkv_all[i], batch, seq,
                [pltpu.VMEM((8, LANES), jnp.float32)])
        else:
            qa, ka, v, hm = _proj_call(
                functools.partial(_moba_proj_kernel, slopes=slopes), "moba_proj",
                h, gain, w_in_moba[j].astype(bf16), [], mkv_all[i], batch, seq,
                [pltpu.VMEM((N_SELF_HEADS, LANES, LANES), jnp.float32)])
        hs = _flash(qa, ka, v, batch, seq)
        wo = w_out[i].astype(bf16)
        h = _ffn(hs, hm, h, wo[:SELF_WIDTH], wo[SELF_WIDTH:],
                 norm_ffn[i].reshape(1, D_MODEL), w_gate_up[i].astype(bf16),
                 w_down[i].astype(bf16), gf, final=(i == depth - 1))
    return h.reshape(batch, seq, D_MODEL)
```

```python
import functools

import jax
import jax.numpy as jnp
from jax import lax
from jax.experimental import pallas as pl
from jax.experimental.pallas import tpu as pltpu

D_MODEL = 1024
N_SELF_HEADS = 12
N_MEM_HEADS = 4
HEAD_DIM = 64
SELF_WIDTH = N_SELF_HEADS * HEAD_DIM
MEM_WIDTH = N_MEM_HEADS * HEAD_DIM
N_MEM = 256
D_FF = 2816
MOBA_BLOCK = 256
MOBA_TOPK = 3
RMS_EPS = 1e-6
NEG = -1e30
BELOW_NEG = -3e38

LANES = 128
N_PAIRS = N_SELF_HEADS // 2
AUG_WIDTH = N_SELF_HEADS * LANES
TOKEN_TILE = MOBA_BLOCK
FFN_TILE = 256
N_BIAS_LANES = 6
MAX_BLOCKS = 32
SCALE = HEAD_DIM ** -0.5
LOG2E = 1.4426950408889634
V_ROWS = 80
VMEM_LIMIT = 56 * 1024 * 1024

Q_OFF, K_OFF, V_OFF = 0, SELF_WIDTH, 2 * SELF_WIDTH
FOX_F_OFF = 3 * SELF_WIDTH
FOX_QM_OFF = FOX_F_OFF + LANES
FOX_PROJ_PAD = FOX_QM_OFF + MEM_WIDTH
MOBA_QM_OFF = 3 * SELF_WIDTH
MOBA_PROJ = MOBA_QM_OFF + MEM_WIDTH


def _nt_dot(a, b):
    return lax.dot_general(a, b, (((1,), (1,)), ((), ())),
                           preferred_element_type=jnp.float32)


def _dot(a, b):
    return jnp.dot(a, b, preferred_element_type=jnp.float32)


def _rms(xf, g):
    ms = jnp.mean(xf * xf, axis=-1, keepdims=True)
    return xf * lax.rsqrt(ms + RMS_EPS) * g


def _split3(x):
    hi = x.astype(jnp.bfloat16).astype(jnp.float32)
    r = x - hi
    mid = r.astype(jnp.bfloat16).astype(jnp.float32)
    return hi, mid, r - mid


def _bias_lanes(lane, base, q_terms, k_terms):
    zero = jnp.zeros(lane.shape, jnp.float32)
    ext_q = jnp.where((lane >= base + 3) & (lane < base + 6), 1.0, zero)
    ext_k = jnp.where((lane >= base) & (lane < base + 3), 1.0, zero)
    for c in range(3):
        ext_q = jnp.where(lane == base + c, q_terms[c], ext_q)
        ext_k = jnp.where(lane == base + 3 + c, k_terms[c], ext_k)
    return ext_q, ext_k


def _memory_attention(qm, mk_ref, mv_ref, hm_ref, lane):
    for p in range(N_MEM_HEADS // 2):
        blk = qm[:, LANES * p:LANES * (p + 1)]
        mk = mk_ref[:, LANES * p:LANES * (p + 1)]
        mv = mv_ref[:, LANES * p:LANES * (p + 1)]
        outs = []
        for e in range(2):
            in_head = (lane >= HEAD_DIM * e) & (lane < HEAD_DIM * (e + 1))
            qh = jnp.where(in_head, blk, 0.0).astype(jnp.bfloat16)
            s = _nt_dot(qh, mk) * SCALE
            s = s - jnp.max(s, axis=-1, keepdims=True)
            pr = jnp.exp(s)
            den = jnp.sum(pr, axis=-1, keepdims=True)
            outs.append(_dot(pr.astype(jnp.bfloat16), mv) / den)
        out = jnp.where(lane < HEAD_DIM, outs[0], outs[1])
        hm_ref[:, LANES * p:LANES * (p + 1)] = out.astype(hm_ref.dtype)


def _write_vt(vt_ref, wvt_ref, xn):
    tm = xn.shape[0]
    vt = _nt_dot(wvt_ref[...], xn)
    pad_rows = V_ROWS - HEAD_DIM
    ones_blk = jnp.where(lax.broadcasted_iota(jnp.int32, (pad_rows, tm), 0) == 0, 1.0, 0.0)
    for hd in range(N_SELF_HEADS):
        vt_ref[0, V_ROWS * hd:V_ROWS * hd + HEAD_DIM, :] = (
            vt[HEAD_DIM * hd:HEAD_DIM * (hd + 1)].astype(vt_ref.dtype))
        vt_ref[0, V_ROWS * hd + HEAD_DIM:V_ROWS * (hd + 1), :] = ones_blk.astype(vt_ref.dtype)


def _fox_proj_kernel(h_ref, g_ref, w_ref, wvt_ref, b_ref, mk_ref, mv_ref,
                     qa_ref, ka_ref, vt_ref, hm_ref, carry_ref):
    i = pl.program_id(1)
    tm = h_ref.shape[0]
    xn = _rms(h_ref[...], g_ref[...]).astype(jnp.bfloat16)
    lane = lax.broadcasted_iota(jnp.int32, (tm, LANES), 1)

    _write_vt(vt_ref, wvt_ref, xn)
    qm = _dot(xn, w_ref[:, FOX_QM_OFF:FOX_QM_OFF + MEM_WIDTH])
    _memory_attention(qm, mk_ref, mv_ref, hm_ref, lane)

    f_logit = _dot(xn, w_ref[:, FOX_F_OFF:FOX_F_OFF + LANES]) + b_ref[...]
    log_f = jnp.minimum(f_logit, 0.0) - jnp.log(1.0 + jnp.exp(-jnp.abs(f_logit)))
    row = lax.broadcasted_iota(jnp.int32, (tm, tm), 0)
    col = lax.broadcasted_iota(jnp.int32, (tm, tm), 1)
    tri = jnp.where(row >= col, 1.0, 0.0).astype(jnp.bfloat16)
    hi, mid, lo = _split3(log_f)
    parts = jnp.concatenate([hi, mid, lo], axis=1).astype(jnp.bfloat16)
    cs = _dot(tri, parts)
    local = cs[:, :LANES] + cs[:, LANES:2 * LANES] + cs[:, 2 * LANES:]

    @pl.when(i == 0)
    def _():
        carry_ref[...] = jnp.zeros_like(carry_ref)

    f_cum = local + carry_ref[0:1, :]
    carry_ref[0:1, :] = f_cum[tm - 1:tm, :]
    f_hi, f_mid, f_lo = _split3(f_cum * LOG2E)

    q = _dot(xn, w_ref[:, Q_OFF:Q_OFF + SELF_WIDTH])
    k = _dot(xn, w_ref[:, K_OFF:K_OFF + SELF_WIDTH])
    for p in range(N_PAIRS):
        qp = q[:, LANES * p:LANES * (p + 1)] * (SCALE * LOG2E)
        kp = k[:, LANES * p:LANES * (p + 1)]
        for e in range(2):
            hd = 2 * p + e
            base = HEAD_DIM * (1 - e)
            in_head = (lane >= HEAD_DIM * e) & (lane < HEAD_DIM * (e + 1))
            cols = [t[:, hd:hd + 1] for t in (f_hi, f_mid, f_lo)]
            ext_q, ext_k = _bias_lanes(lane, base, cols, [-c for c in cols])
            sl = slice(LANES * hd, LANES * (hd + 1))
            qa_ref[:, sl] = jnp.where(in_head, qp, ext_q).astype(qa_ref.dtype)
            ka_ref[:, sl] = jnp.where(in_head, kp, ext_k).astype(ka_ref.dtype)


def _moba_proj_kernel(h_ref, g_ref, w_ref, wvt_ref, mk_ref, mv_ref,
                      qa_ref, ka_ref, vt_ref, hm_ref, km_ref, *, slopes):
    i = pl.program_id(1)
    tm = h_ref.shape[0]
    xn = _rms(h_ref[...], g_ref[...]).astype(jnp.bfloat16)
    lane = lax.broadcasted_iota(jnp.int32, (tm, LANES), 1)
    lane_row = lax.broadcasted_iota(jnp.int32, (1, LANES), 1)

    _write_vt(vt_ref, wvt_ref, xn)
    qm =_dot(xn, w_ref[:, MOBA_QM_OFF:MOBA_QM_OFF + MEM_WIDTH])
    _memory_attention(qm, mk_ref, mv_ref, hm_ref, lane)

    @pl.when(i == 0)
    def _():
        km_ref[...] = jnp.zeros_like(km_ref)

    q = _dot(xn, w_ref[:, Q_OFF:Q_OFF + SELF_WIDTH])
    k = _dot(xn, w_ref[:, K_OFF:K_OFF + SELF_WIDTH])
    kmean = jnp.mean(k, axis=0, keepdims=True)
    pos = (i * tm + lax.broadcasted_iota(jnp.int32, (tm, 1), 0)).astype(jnp.float32)

    for p in range(N_PAIRS):
        qp = q[:, LANES * p:LANES * (p + 1)]
        kp = k[:, LANES * p:LANES * (p + 1)]
        q_hi = qp.astype(jnp.bfloat16)
        q_lo = (qp - q_hi.astype(jnp.float32)).astype(jnp.bfloat16)
        for e in range(2):
            hd = 2 * p + e
            base = HEAD_DIM * (1 - e)
            sel0 = base + N_BIAS_LANES
            in_head = (lane >= HEAD_DIM * e) & (lane < HEAD_DIM * (e + 1))

            km = km_ref[hd]
            km_hi = km.astype(jnp.bfloat16)
            km_lo = (km - km_hi.astype(jnp.float32)).astype(jnp.bfloat16)
            gate = _nt_dot(q_hi, km_hi) + _nt_dot(q_lo, km_hi) + _nt_dot(q_hi, km_lo)
            g = jnp.where((lane >= sel0) & (lane < sel0 + i), gate, BELOW_NEG)
            chosen = lane == sel0 + i
            for _ in range(MOBA_TOPK):
                best = jnp.max(g, axis=-1, keepdims=True)
                first = jnp.min(jnp.where(g == best, lane, 2 * LANES),
                                axis=-1, keepdims=True)
                hit = lane == first
                chosen = chosen | (hit & (best > 0.5 * BELOW_NEG))
                g = jnp.where(hit, BELOW_NEG, g)
            pen = jnp.where(chosen, 0.0, NEG)

            slope = slopes[hd] * LOG2E
            q_terms = _split3(-slope * pos)
            k_terms = _split3(slope * pos)
            ext_q, ext_k = _bias_lanes(lane, base, q_terms, k_terms)
            in_sel = (lane >= sel0) & (lane < sel0 + MAX_BLOCKS)
            ext_q = jnp.where(in_sel, pen, ext_q)
            ext_k = jnp.where(lane == sel0 + i, 1.0, ext_k)
            sl = slice(LANES * hd, LANES * (hd + 1))
            qa_ref[:, sl] = jnp.where(in_head, qp * (SCALE * LOG2E), ext_q).astype(qa_ref.dtype)
            ka_ref[:, sl] = jnp.where(in_head, kp, ext_k).astype(ka_ref.dtype)

            in_head_row = (lane_row >= HEAD_DIM * e) & (lane_row < HEAD_DIM * (e + 1))
            km_row = jnp.where(in_head_row, kmean[:, LANES * p:LANES * (p + 1)], 0.0)
            km_ref[hd, pl.ds(sel0 + i, 1), :] = km_row


def _flash_kernel(qa_ref, ka_ref, vt_ref, kpad_ref, o_ref,
                  kbuf, m_ref, acc_ref, s0, s1, p0, p1, a0, a1):
    i = pl.program_id(2)
    nq = pl.num_programs(2)
    tq = qa_ref.shape[0]
    tk = vt_ref.shape[2]
    seq = ka_ref.shape[0]

    @pl.when(i == 0)
    def _():
        kbuf[0:seq, :] = ka_ref[...]
        kbuf[seq:seq + tk, :] = kpad_ref[...]

    def scores(tile, s_ref):
        rows = pl.ds(pl.multiple_of(tile * tk, tk), tk)
        for e in range(2):
            s_ref[e] = _nt_dot(kbuf[rows, LANES * e:LANES * (e + 1)],
                               qa_ref[:, LANES * e:LANES * (e + 1)])

    def softmax(s_ref, p_ref, a_ref, causal=False):
        for e in range(2):
            st = s_ref[e]
            if causal:
                kv = lax.broadcasted_iota(jnp.int32, (tk, tq), 0)
                qi = lax.broadcasted_iota(jnp.int32, (tk, tq), 1)
                st = jnp.where(kv <= qi, st, NEG)
            m_prev = m_ref[e]
            m_new = jnp.maximum(m_prev, jnp.max(st, axis=0, keepdims=True))
            a_ref[e] = jnp.exp2(m_prev - m_new)
            p_ref[e] = jnp.exp2(st - m_new).astype(p_ref.dtype)
            m_ref[e] = m_new

    def values(tile, p_ref, a_ref):
        for e in range(2):
            vt = vt_ref[tile, V_ROWS * e:V_ROWS * (e + 1), :]
            acc_ref[e] = a_ref[e] * acc_ref[e] + _dot(vt, p_ref[e])

    def key_tile(pos):
        return jnp.where(pos < i, pos, nq)

    def value_tile(pos):
        return jnp.where(pos < 0, i, jnp.minimum(pos, nq - 1))

    m_ref[...] = jnp.full(m_ref.shape, NEG, jnp.float32)
    acc_ref[...] = jnp.zeros(acc_ref.shape, jnp.float32)
    scores(i, s1)
    scores(key_tile(0), s0)
    softmax(s1, p1, a1, causal=True)

    def body(t, carry):
        pos = 2 * t
        scores(key_tile(pos + 1), s1)
        values(value_tile(pos - 1), p1, a1)
        softmax(s0, p0, a0)
        scores(key_tile(pos + 2), s0)
        values(value_tile(pos), p0, a0)
        softmax(s1, p1, a1)
        return carry

    lax.fori_loop(0, (i + 2) // 2, body, 0)

    outs = []
    for e in range(2):
        acc = acc_ref[e]
        outs.append(acc[:HEAD_DIM] / acc[HEAD_DIM:HEAD_DIM + 1])
    o_ref[...] = jnp.concatenate(outs, axis=0).T.astype(o_ref.dtype)


def _ffn_kernel(hs_ref, hm_ref, h_ref, woa_ref, wob_ref, g_ref, wgu_ref, wd_ref,
                gf_ref, o_ref, *, final):
    h1 = h_ref[...] + _dot(hs_ref[...], woa_ref[...]) + _dot(hm_ref[...], wob_ref[...])
    hn = _rms(h1, g_ref[...]).astype(jnp.bfloat16)
    ffn = None
    for c0, c1 in ((0, 1536), (1536, D_FF)):
        gate = _dot(hn, wgu_ref[:, c0:c1])
        up = _dot(hn, wgu_ref[:, D_FF + c0:D_FF + c1])
        act = (gate * jax.nn.sigmoid(gate) * up).astype(jnp.bfloat16)
        down = _dot(act, wd_ref[c0:c1, :])
        ffn = down if ffn is None else ffn + down
    y = h1 + ffn
    if final:
        y = _rms(y, gf_ref[...])
    o_ref[...] = y


def _mem_kv_kernel(mem_ref, g_ref, w_ref, o_ref):
    mn = _rms(mem_ref[...], g_ref[0]).astype(jnp.bfloat16)
    o_ref[0] = _dot(mn, w_ref[0]).astype(o_ref.dtype)


def _params(sem, flags=None):
    return pltpu.CompilerParams(dimension_semantics=sem, vmem_limit_bytes=VMEM_LIMIT,
                                flags=flags)


def _const_spec(shape):
    nd = len(shape)
    return pl.BlockSpec(shape, lambda *_: (0,) * nd)


def _mem_kv(mem2, norm_mem, w_mem_kv, batch):
    depth = w_mem_kv.shape[0]
    return pl.pallas_call(
        _mem_kv_kernel,
        grid=(depth, batch),
        in_specs=[pl.BlockSpec((N_MEM, D_MODEL), lambda d, b: (b, 0)),
                  pl.BlockSpec((1, 1, D_MODEL), lambda d, b: (d, 0, 0)),
                  pl.BlockSpec((1, D_MODEL, 2 * MEM_WIDTH), lambda d, b: (d, 0, 0))],
        out_specs=pl.BlockSpec((1, N_MEM, 2 * MEM_WIDTH), lambda d, b: (d, b, 0)),
        out_shape=jax.ShapeDtypeStruct((depth, batch * N_MEM, 2 * MEM_WIDTH), jnp.bfloat16),
        compiler_params=_params(("arbitrary", "arbitrary")),
        name="mem_kv",
    )(mem2, norm_mem.reshape(depth, 1, D_MODEL), w_mem_kv)


def _proj_call(kernel_fn, name, h, gain, w, extra, mkv, batch, seq, scratch):
    nt = seq // TOKEN_TILE
    tokens = batch * seq
    tok = lambda b, i: (b * nt + i, 0)
    wvt = w[:, V_OFF:V_OFF + SELF_WIDTH].T
    in_specs = [pl.BlockSpec((TOKEN_TILE, D_MODEL), tok),
                _const_spec((1, D_MODEL)),
                _const_spec(w.shape),
                _const_spec(wvt.shape)]
    args = [h, gain, w, wvt]
    for a in extra:
        in_specs.append(_const_spec(a.shape))
        args.append(a)
    in_specs += [pl.BlockSpec((N_MEM, MEM_WIDTH), lambda b, i: (b, 0)),
                 pl.BlockSpec((N_MEM, MEM_WIDTH), lambda b, i: (b, 1))]
    args += [mkv, mkv]
    out_specs = [pl.BlockSpec((TOKEN_TILE, AUG_WIDTH), tok),
                 pl.BlockSpec((TOKEN_TILE, AUG_WIDTH), tok),
                 pl.BlockSpec((1, N_SELF_HEADS * V_ROWS, TOKEN_TILE),
                              lambda b, i: (b * nt + i, 0, 0)),
                 pl.BlockSpec((TOKEN_TILE, MEM_WIDTH), tok)]
    out_shape = [jax.ShapeDtypeStruct((tokens, AUG_WIDTH), jnp.bfloat16),
                 jax.ShapeDtypeStruct((tokens, AUG_WIDTH), jnp.bfloat16),
                 jax.ShapeDtypeStruct((tokens // TOKEN_TILE, N_SELF_HEADS * V_ROWS, TOKEN_TILE),
                                      jnp.bfloat16),
                 jax.ShapeDtypeStruct((tokens, MEM_WIDTH), jnp.bfloat16)]
    return pl.pallas_call(
        kernel_fn,
        grid=(batch, nt),
        in_specs=in_specs,
        out_specs=out_specs,
        out_shape=out_shape,
        scratch_shapes=scratch,
        compiler_params=_params(("arbitrary", "arbitrary")),
        name=name,
    )(*args)


def _flash(qa, ka, vt, batch, seq):
    nq = seq // TOKEN_TILE
    tokens = batch * seq
    tq = TOKEN_TILE
    f32, bf16 = jnp.float32, jnp.bfloat16
    lane = jnp.arange(2 * LANES)
    ones_lane = (lane == HEAD_DIM + 3) | (lane == LANES + 3)
    kpad = jnp.broadcast_to(jnp.where(ones_lane, NEG, 0.0), (tq, 2 * LANES)).astype(bf16)
    return pl.pallas_call(
        _flash_kernel,
        grid=(batch, N_PAIRS, nq),
        in_specs=[pl.BlockSpec((tq, 2 * LANES), lambda b, p, i: (b * nq + i, p)),
                  pl.BlockSpec((seq, 2 * LANES), lambda b, p, i: (b, p)),
                  pl.BlockSpec((nq, 2 * V_ROWS, tq), lambda b, p, i: (b, p, 0)),
                  _const_spec(kpad.shape)],
        out_specs=pl.BlockSpec((tq, LANES), lambda b, p, i: (b * nq + i, p)),
        out_shape=jax.ShapeDtypeStruct((tokens, SELF_WIDTH), bf16),
        scratch_shapes=[pltpu.VMEM((seq + tq, 2 * LANES), bf16),
                        pltpu.VMEM((2, 1, tq), f32), pltpu.VMEM((2, V_ROWS, tq), f32),
                        pltpu.VMEM((2, tq, tq), f32), pltpu.VMEM((2, tq, tq), f32),
                        pltpu.VMEM((2, tq, tq), bf16), pltpu.VMEM((2, tq, tq), bf16),
                        pltpu.VMEM((2, 1, tq), f32), pltpu.VMEM((2, 1, tq), f32)],
        compiler_params=_params(("arbitrary", "arbitrary", "arbitrary")),
        name="flash_attention",
    )(qa, ka, vt, kpad)


def _ffn(hs, hm, h, woa, wob, g, wgu, wd, gf, final):
    tokens = h.shape[0]
    tok = lambda t: (t, 0)
    return pl.pallas_call(
        functools.partial(_ffn_kernel, final=final),
        grid=(tokens // FFN_TILE,),
        in_specs=[pl.BlockSpec((FFN_TILE, SELF_WIDTH), tok),
                  pl.BlockSpec((FFN_TILE, MEM_WIDTH), tok),
                  pl.BlockSpec((FFN_TILE, D_MODEL), tok),
                  _const_spec(woa.shape), _const_spec(wob.shape),
                  _const_spec((1, D_MODEL)),
                  _const_spec(wgu.shape), _const_spec(wd.shape),
                  _const_spec((1, D_MODEL))],
        out_specs=pl.BlockSpec((FFN_TILE, D_MODEL), tok),
        out_shape=jax.ShapeDtypeStruct((tokens, D_MODEL), jnp.float32),
        compiler_params=_params(("arbitrary",)),
        name="out_proj_ffn",
    )(hs, hm, h, woa, wob, g, wgu, wd, gf)


def kernel(x, mem, norm_mix, norm_mem, norm_ffn, norm_final, w_in_fox, b_fgate,
           w_in_moba, w_mem_kv, w_out, w_gate_up, w_down):
    batch, seq, _ = x.shape
    depth = w_out.shape[0]
    bf16 = jnp.bfloat16
    assert seq % TOKEN_TILE == 0 and seq // MOBA_BLOCK <= MAX_BLOCKS
    assert (batch * seq) % FFN_TILE == 0

    slopes = tuple(2.0 ** (-8.0 * (hd + 1) / N_SELF_HEADS) for hd in range(N_SELF_HEADS))
    h = x.reshape(batch * seq, D_MODEL)
    mkv_all = _mem_kv(mem.reshape(batch * N_MEM, D_MODEL), norm_mem,
                      w_mem_kv.astype(bf16), batch)
    gf = norm_final.reshape(1, D_MODEL)

    for i in range(depth):
        j = i // 2
        gain = norm_mix[i].reshape(1, D_MODEL)
        if i % 2 == 0:
            w = w_in_fox[j]
            f_cols = jnp.pad(w[:, FOX_F_OFF:FOX_F_OFF + N_SELF_HEADS],
                             ((0, 0), (0, LANES - N_SELF_HEADS)))
            w = jnp.concatenate([w[:, :FOX_F_OFF], f_cols,
                                 w[:, FOX_F_OFF + N_SELF_HEADS:]], axis=1).astype(bf16)
            bias = jnp.pad(b_fgate[j], (0, LANES - N_SELF_HEADS)).reshape(1, LANES)
            qa, ka, v, hm = _proj_call(
                _fox_proj_kernel, "fox_proj", h, gain, w, [bias], mkv_all[i], batch, seq,
                [pltpu.VMEM((8, LANES), jnp.float32)])
        else:
            qa, ka, v, hm = _proj_call(
                functools.partial(_moba_proj_kernel, slopes=slopes), "moba_proj",
                h, gain, w_in_moba[j].astype(bf16), [], mkv_all[i], batch, seq,
                [pltpu.VMEM((N_SELF_HEADS, LANES, LANES), jnp.float32)])
        hs = _flash(qa, ka, v, batch, seq)
        wo = w_out[i].astype(bf16)
        h = _ffn(hs, hm, h, wo[:SELF_WIDTH], wo[SELF_WIDTH:],
                 norm_ffn[i].reshape(1, D_MODEL), w_gate_up[i].astype(bf16),
                 w_down[i].astype(bf16), gf, final=(i == depth - 1))
    return h.reshape(batch, seq, D_MODEL)
```

```python
import functools

import jax
import jax.numpy as jnp
from jax import lax
from jax.experimental import pallas as pl
from jax.experimental.pallas import tpu as pltpu

D_MODEL = 1024
N_SELF_HEADS = 12
N_MEM_HEADS = 4
HEAD_DIM = 64
SELF_WIDTH = N_SELF_HEADS * HEAD_DIM
MEM_WIDTH = N_MEM_HEADS * HEAD_DIM
N_MEM = 256
D_FF = 2816
MOBA_BLOCK = 256
MOBA_TOPK = 3
RMS_EPS = 1e-6
NEG = -1e30
BELOW_NEG = -3e38

LANES = 128
N_PAIRS = N_SELF_HEADS // 2
AUG_WIDTH = N_SELF_HEADS * LANES
TOKEN_TILE = MOBA_BLOCK
FFN_TILE = 256
MAX_BLOCKS = 32
SEL_ROW = 16
SCALE = HEAD_DIM ** -0.5
LOG2E = 1.4426950408889634
V_ROWS = 80
VMEM_LIMIT = 56 * 1024 * 1024

Q_OFF, K_OFF, V_OFF = 0, SELF_WIDTH, 2 * SELF_WIDTH
FOX_F_OFF = 3 * SELF_WIDTH
FOX_QM_OFF = FOX_F_OFF + LANES
FOX_PROJ_PAD = FOX_QM_OFF + MEM_WIDTH
MOBA_QM_OFF = 3 * SELF_WIDTH
MOBA_PROJ = MOBA_QM_OFF + MEM_WIDTH


def _nt_dot(a, b):
    return lax.dot_general(a, b, (((1,), (1,)), ((), ())),
                           preferred_element_type=jnp.float32)


def _dot(a, b):
    return jnp.dot(a, b, preferred_element_type=jnp.float32)


def _rms(xf, g):
    ms = jnp.mean(xf * xf, axis=-1, keepdims=True)
    return xf * lax.rsqrt(ms + RMS_EPS) * g


def _split3(x):
    hi = x.astype(jnp.bfloat16).astype(jnp.float32)
    r = x - hi
    mid = r.astype(jnp.bfloat16).astype(jnp.float32)
    return hi, mid, r - mid


def _bias_lanes(lane, base, q_terms, k_terms):
    zero = jnp.zeros(lane.shape, jnp.float32)
    ext_q = jnp.where((lane >= base + 3) & (lane < base + 6), 1.0, zero)
    ext_k = jnp.where((lane >= base) & (lane < base + 3), 1.0, zero)
    for c in range(3):
        ext_q = jnp.where(lane == base + c, q_terms[c], ext_q)
        ext_k = jnp.where(lane == base + 3 + c, k_terms[c], ext_k)
    return ext_q, ext_k


def _memory_attention(qm, mk_ref, mv_ref, hm_ref, lane):
    for p in range(N_MEM_HEADS // 2):
        blk = qm[:, LANES * p:LANES * (p + 1)]
        mk = mk_ref[:, LANES * p:LANES * (p + 1)]
        mv = mv_ref[:, LANES * p:LANES * (p + 1)]
        outs = []
        for e in range(2):
            in_head = (lane >= HEAD_DIM * e) & (lane < HEAD_DIM * (e + 1))
            qh = jnp.where(in_head, blk, 0.0).astype(jnp.bfloat16)
            s = _nt_dot(qh, mk) * SCALE
            s = s - jnp.max(s, axis=-1, keepdims=True)
            pr = jnp.exp(s)
            den = jnp.sum(pr, axis=-1, keepdims=True)
            outs.append(_dot(pr.astype(jnp.bfloat16), mv) / den)
        out = jnp.where(lane < HEAD_DIM, outs[0], outs[1])
        hm_ref[:, LANES * p:LANES * (p + 1)] = out.astype(hm_ref.dtype)


def _write_vt(vt_ref, vt):
    tm = vt.shape[1]
    pad_rows = V_ROWS - HEAD_DIM
    ones_blk = jnp.where(lax.broadcasted_iota(jnp.int32, (pad_rows, tm), 0) == 0, 1.0, 0.0)
    for hd in range(N_SELF_HEADS):
        vt_ref[0, V_ROWS * hd:V_ROWS * hd + HEAD_DIM, :] = (
            vt[HEAD_DIM * hd:HEAD_DIM * (hd + 1)].astype(vt_ref.dtype))
        vt_ref[0, V_ROWS * hd + HEAD_DIM:V_ROWS * (hd + 1), :] = ones_blk.astype(vt_ref.dtype)


def _fox_proj_kernel(h_ref, g_ref, w_ref, wvt_ref, b_ref, mk_ref, mv_ref,
                     qa_ref, ka_ref, vt_ref, hm_ref, carry_ref):
    i = pl.program_id(1)
    tm = h_ref.shape[0]
    xn = _rms(h_ref[...], g_ref[...]).astype(jnp.bfloat16)
    lane = lax.broadcasted_iota(jnp.int32, (tm, LANES), 1)

    _write_vt(vt_ref, _nt_dot(wvt_ref[...], xn))
    qm = _dot(xn, w_ref[:, FOX_QM_OFF:FOX_QM_OFF + MEM_WIDTH])
    _memory_attention(qm, mk_ref, mv_ref, hm_ref, lane)

    f_logit = _dot(xn, w_ref[:, FOX_F_OFF:FOX_F_OFF + LANES]) + b_ref[...]
    log_f = jnp.minimum(f_logit, 0.0) - jnp.log(1.0 + jnp.exp(-jnp.abs(f_logit)))
    row = lax.broadcasted_iota(jnp.int32, (tm, tm), 0)
    col = lax.broadcasted_iota(jnp.int32, (tm, tm), 1)
    tri = jnp.where(row >= col, 1.0, 0.0).astype(jnp.bfloat16)
    hi, mid, lo = _split3(log_f)
    parts = jnp.concatenate([hi, mid, lo], axis=1).astype(jnp.bfloat16)
    cs = _dot(tri, parts)
    local = cs[:, :LANES] + cs[:, LANES:2 * LANES] + cs[:, 2 * LANES:]

    @pl.when(i == 0)
    def _():
        carry_ref[...] = jnp.zeros_like(carry_ref)

    f_cum = local + carry_ref[0:1, :]
    carry_ref[0:1, :] = f_cum[tm - 1:tm, :]
    f_hi, f_mid, f_lo = _split3(f_cum * LOG2E)

    q = _dot(xn, w_ref[:, Q_OFF:Q_OFF + SELF_WIDTH])
    k = _dot(xn, w_ref[:, K_OFF:K_OFF + SELF_WIDTH])
    for p in range(N_PAIRS):
        qp = q[:, LANES * p:LANES * (p + 1)] * (SCALE * LOG2E)
        kp = k[:, LANES * p:LANES * (p + 1)]
        for e in range(2):
            hd = 2 * p + e
            base = HEAD_DIM * (1 - e)
            in_head = (lane >= HEAD_DIM * e) & (lane < HEAD_DIM * (e + 1))
            cols = [t[:, hd:hd + 1] for t in (f_hi, f_mid, f_lo)]
            ext_q, ext_k = _bias_lanes(lane, base, cols, [-c for c in cols])
            sl = slice(LANES * hd, LANES * (hd + 1))
            qa_ref[:, sl] = jnp.where(in_head, qp, ext_q).astype(qa_ref.dtype)
            ka_ref[:, sl] = jnp.where(in_head, kp, ext_k).astype(ka_ref.dtype)


def _moba_proj_kernel(h_ref, g_ref, w_ref, wqvt_ref, mk_ref, mv_ref,
                      qt_ref, ka_ref, vt_ref, hm_ref, km_ref, *, slopes):
    i = pl.program_id(1)
    tm = h_ref.shape[0]
    xn = _rms(h_ref[...], g_ref[...]).astype(jnp.bfloat16)
    lane = lax.broadcasted_iota(jnp.int32, (tm, LANES), 1)
    lane_row = lax.broadcasted_iota(jnp.int32, (1, LANES), 1)

    qvt = _nt_dot(wqvt_ref[...], xn)
    _write_vt(vt_ref, qvt[SELF_WIDTH:])
    qm = _dot(xn, w_ref[:, MOBA_QM_OFF:MOBA_QM_OFF + MEM_WIDTH])
    _memory_attention(qm, mk_ref, mv_ref, hm_ref, lane)

    @pl.when(i == 0)
    def _():
        km_ref[...] = jnp.zeros_like(km_ref)

    k = _dot(xn, w_ref[:, K_OFF:K_OFF + SELF_WIDTH])
    kmean = jnp.mean(k, axis=0, keepdims=True)

    block_start = (i * tm).astype(jnp.float32)
    offset = lax.broadcasted_iota(jnp.int32, (tm, LANES), 0).astype(jnp.float32)
    ext_k = []
    for e in range(2):
        base = HEAD_DIM * (1 - e)
        x = jnp.where((lane >= base) & (lane < base + 3), 1.0, 0.0)
        x = jnp.where((lane >= base + 4) & (lane < base + 7), block_start, x)
        x = jnp.where((lane >= base + 7) & (lane < base + 10), offset, x)
        ext_k.append(jnp.where(lane == base + SEL_ROW + i, 1.0, x))

    pos_row = (i * tm + lax.broadcasted_iota(jnp.int32, (1, tm), 1)).astype(jnp.float32)
    blk = lax.broadcasted_iota(jnp.int32, (MAX_BLOCKS, tm), 0)
    brow = lax.broadcasted_iota(jnp.int32, (SEL_ROW, tm), 0)
    tail = jnp.zeros((HEAD_DIM - SEL_ROW - MAX_BLOCKS, tm), jnp.float32)

    for p in range(N_PAIRS):
        qt_pair = qvt[LANES * p:LANES * (p + 1)]
        kp = k[:, LANES * p:LANES * (p + 1)]
        q_hi = qt_pair.astype(jnp.bfloat16)
        q_lo = (qt_pair - q_hi.astype(jnp.float32)).astype(jnp.bfloat16)
        km = km_ref[p]
        km_hi = km.astype(jnp.bfloat16)
        km_lo = (km - km_hi.astype(jnp.float32)).astype(jnp.bfloat16)
        r = _dot(jnp.concatenate([km_hi, km_lo], axis=0), q_hi)
        gates = r[:2 * MAX_BLOCKS] + r[2 * MAX_BLOCKS:] + _dot(km_hi, q_lo)
        for e in range(2):
            hd = 2 * p + e
            g = jnp.where(blk < i, gates[MAX_BLOCKS * e:MAX_BLOCKS * (e + 1)], BELOW_NEG)
            chosen = blk == i
            for _ in range(MOBA_TOPK):
                best = jnp.max(g, axis=0, keepdims=True)
                first = jnp.min(jnp.where(g == best, blk, 2 * MAX_BLOCKS),
                                axis=0, keepdims=True)
                hit = blk == first
                chosen = chosen | (hit & (best > 0.5 * BELOW_NEG))
                g = jnp.where(hit, BELOW_NEG, g)
            pen = jnp.where(chosen, 0.0, NEG)

            slope = slopes[hd] * LOG2E
            t_terms = _split3(-slope * pos_row)
            s_terms = _split3(jnp.full((1, tm), slope, jnp.float32))
            bias = jnp.where(brow == 3, 1.0, 0.0)
            for c in range(3):
                bias = jnp.where(brow == c, t_terms[c], bias)
                bias = jnp.where((brow == 4 + c) | (brow == 7 + c), s_terms[c], bias)
            ext_q = jnp.concatenate([bias, pen, tail], axis=0)
            data = qt_pair[HEAD_DIM * e:HEAD_DIM * (e + 1)] * (SCALE * LOG2E)
            parts = [data, ext_q] if e == 0 else [ext_q, data]
            qt_ref[0, LANES * hd:LANES * (hd + 1), :] = (
                jnp.concatenate(parts, axis=0).astype(qt_ref.dtype))

            in_head = (lane >= HEAD_DIM * e) & (lane < HEAD_DIM * (e + 1))
            ka_ref[:, LANES * hd:LANES * (hd + 1)] = (
                jnp.where(in_head, kp, ext_k[e]).astype(ka_ref.dtype))

            in_head_row = (lane_row >= HEAD_DIM * e) & (lane_row < HEAD_DIM * (e + 1))
            km_row = jnp.where(in_head_row, kmean[:, LANES * p:LANES * (p + 1)], 0.0)
            km_ref[p, pl.ds(MAX_BLOCKS * e + i, 1), :] = km_row


def _flash_kernel(qa_ref, ka_ref, vt_ref, kpad_ref, o_ref,
                  kbuf, m_ref, acc_ref, s0, s1, p0, p1, a0, a1, *, q_feature_major):
    i = pl.program_id(2)
    nq = pl.num_programs(2)
    tq = o_ref.shape[0]
    tk = vt_ref.shape[2]
    seq = ka_ref.shape[0]

    @pl.when(i == 0)
    def _():
        kbuf[0:seq, :] = ka_ref[...]
        kbuf[seq:seq + tk, :] = kpad_ref[...]

    def scores(tile, s_ref):
        rows = pl.ds(pl.multiple_of(tile * tk, tk), tk)
        for e in range(2):
            k = kbuf[rows, LANES * e:LANES * (e + 1)]
            if q_feature_major:
                s_ref[e] = _dot(k, qa_ref[0, LANES * e:LANES * (e + 1), :])
            else:
                s_ref[e] = _nt_dot(k, qa_ref[:, LANES * e:LANES * (e + 1)])

    def softmax(s_ref, p_ref, a_ref, causal=False):
        for e in range(2):
            st = s_ref[e]
            if causal:
                kv = lax.broadcasted_iota(jnp.int32, (tk, tq), 0)
                qi = lax.broadcasted_iota(jnp.int32, (tk, tq), 1)
                st = jnp.where(kv <= qi, st, NEG)
            m_prev = m_ref[e]
            m_new = jnp.maximum(m_prev, jnp.max(st, axis=0, keepdims=True))
            a_ref[e] = jnp.exp2(m_prev - m_new)
            p_ref[e] = jnp.exp2(st - m_new).astype(p_ref.dtype)
            m_ref[e] = m_new

    def values(tile, p_ref, a_ref):
        for e in range(2):
            vt = vt_ref[tile, V_ROWS * e:V_ROWS * (e + 1), :]
            acc_ref[e] = a_ref[e] * acc_ref[e] + _dot(vt, p_ref[e])

    def key_tile(pos):
        return jnp.where(pos < i, pos, nq)

    def value_tile(pos):
        return jnp.where(pos < 0, i, jnp.minimum(pos, nq - 1))

    m_ref[...] = jnp.full(m_ref.shape, NEG, jnp.float32)
    acc_ref[...] = jnp.zeros(acc_ref.shape, jnp.float32)
    scores(i, s1)
    scores(key_tile(0), s0)
    softmax(s1, p1, a1, causal=True)

    def body(t, carry):
        pos = 2 * t
        scores(key_tile(pos + 1), s1)
        values(value_tile(pos - 1), p1, a1)
        softmax(s0, p0, a0)
        scores(key_tile(pos + 2), s0)
        values(value_tile(pos), p0, a0)
        softmax(s1, p1, a1)
        return carry

    lax.fori_loop(0, (i + 2) // 2, body, 0)

    outs = []
    for e in range(2):
        acc = acc_ref[e]
        outs.append(acc[:HEAD_DIM] / acc[HEAD_DIM:HEAD_DIM + 1])
    o_ref[...] = jnp.concatenate(outs, axis=0).T.astype(o_ref.dtype)


def _ffn_kernel(hs_ref, hm_ref, h_ref, woa_ref, wob_ref, g_ref, wgu_ref, wd_ref,
                gf_ref, o_ref, *, final):
    h1 = h_ref[...] + _dot(hs_ref[...], woa_ref[...]) + _dot(hm_ref[...], wob_ref[...])
    hn = _rms(h1, g_ref[...]).astype(jnp.bfloat16)
    ffn = None
    for c0, c1 in ((0, 1536), (1536, D_FF)):
        gate = _dot(hn, wgu_ref[:, c0:c1])
        up = _dot(hn, wgu_ref[:, D_FF + c0:D_FF + c1])
        act = (gate * jax.nn.sigmoid(gate) * up).astype(jnp.bfloat16)
        down = _dot(act, wd_ref[c0:c1, :])
        ffn = down if ffn is None else ffn + down
    y = h1 + ffn
    if final:
        y = _rms(y, gf_ref[...])
    o_ref[...] = y


def _mem_kv_kernel(mem_ref, g_ref, w_ref, o_ref):
    mn = _rms(mem_ref[...], g_ref[0]).astype(jnp.bfloat16)
    o_ref[0] = _dot(mn, w_ref[0]).astype(o_ref.dtype)


def _params(sem, flags=None):
    return pltpu.CompilerParams(dimension_semantics=sem, vmem_limit_bytes=VMEM_LIMIT,
                                flags=flags)


def _const_spec(shape):
    nd = len(shape)
    return pl.BlockSpec(shape, lambda *_: (0,) * nd)


def _mem_kv(mem2, norm_mem, w_mem_kv, batch):
    depth = w_mem_kv.shape[0]
    return pl.pallas_call(
        _mem_kv_kernel,
        grid=(depth, batch),
        in_specs=[pl.BlockSpec((N_MEM, D_MODEL), lambda d, b: (b, 0)),
                  pl.BlockSpec((1, 1, D_MODEL), lambda d, b: (d, 0, 0)),
                  pl.BlockSpec((1, D_MODEL, 2 * MEM_WIDTH), lambda d, b: (d, 0, 0))],
        out_specs=pl.BlockSpec((1, N_MEM, 2 * MEM_WIDTH), lambda d, b: (d, b, 0)),
        out_shape=jax.ShapeDtypeStruct((depth, batch * N_MEM, 2 * MEM_WIDTH), jnp.bfloat16),
        compiler_params=_params(("arbitrary", "arbitrary")),
        name="mem_kv",
    )(mem2, norm_mem.reshape(depth, 1, D_MODEL), w_mem_kv)


def _proj_call(kernel_fn, name, h, gain, w, wt, extra, mkv, batch, seq, scratch,
               q_feature_major):
    nt = seq // TOKEN_TILE
    tokens = batch * seq
    tok = lambda b, i: (b * nt + i, 0)
    tile = lambda b, i: (b * nt + i, 0, 0)
    bf16 = jnp.bfloat16
    in_specs = [pl.BlockSpec((TOKEN_TILE, D_MODEL), tok),
                _const_spec((1, D_MODEL)),
                _const_spec(w.shape),
                _const_spec(wt.shape)]
    args = [h, gain, w, wt]
    for a in extra:
        in_specs.append(_const_spec(a.shape))
        args.append(a)
    in_specs += [pl.BlockSpec((N_MEM, MEM_WIDTH), lambda b, i: (b, 0)),
                 pl.BlockSpec((N_MEM, MEM_WIDTH), lambda b, i: (b, 1))]
    args += [mkv, mkv]
    if q_feature_major:
        q_spec = pl.BlockSpec((1, AUG_WIDTH, TOKEN_TILE), tile)
        q_shape = jax.ShapeDtypeStruct((tokens // TOKEN_TILE, AUG_WIDTH, TOKEN_TILE), bf16)
    else:
        q_spec = pl.BlockSpec((TOKEN_TILE, AUG_WIDTH), tok)
        q_shape = jax.ShapeDtypeStruct((tokens, AUG_WIDTH), bf16)
    out_specs = [q_spec,
                 pl.BlockSpec((TOKEN_TILE, AUG_WIDTH), tok),
                 pl.BlockSpec((1, N_SELF_HEADS * V_ROWS, TOKEN_TILE), tile),
                 pl.BlockSpec((TOKEN_TILE, MEM_WIDTH), tok)]
    out_shape = [q_shape,
                 jax.ShapeDtypeStruct((tokens, AUG_WIDTH), bf16),
                 jax.ShapeDtypeStruct((tokens // TOKEN_TILE, N_SELF_HEADS * V_ROWS, TOKEN_TILE),
                                      bf16),
                 jax.ShapeDtypeStruct((tokens, MEM_WIDTH), bf16)]
    return pl.pallas_call(
        kernel_fn,
        grid=(batch, nt),
        in_specs=in_specs,
        out_specs=out_specs,
        out_shape=out_shape,
        scratch_shapes=scratch,
        compiler_params=_params(("arbitrary", "arbitrary")),
        name=name,
    )(*args)


def _flash(qa, ka, vt, batch, seq, q_feature_major):
    nq = seq // TOKEN_TILE
    tokens = batch * seq
    tq = TOKEN_TILE
    f32, bf16 = jnp.float32, jnp.bfloat16
    lane = jnp.arange(2 * LANES)
    ones_lane = (lane == HEAD_DIM + 3) | (lane == LANES + 3)
    kpad = jnp.broadcast_to(jnp.where(ones_lane, NEG, 0.0), (tq, 2 * LANES)).astype(bf16)
    if q_feature_major:
        q_spec = pl.BlockSpec((1, 2 * LANES, tq), lambda b, p, i: (b * nq + i, p, 0))
    else:
        q_spec = pl.BlockSpec((tq, 2 * LANES), lambda b, p, i: (b * nq + i, p))
    return pl.pallas_call(
        functools.partial(_flash_kernel, q_feature_major=q_feature_major),
        grid=(batch, N_PAIRS, nq),
        in_specs=[q_spec,
                  pl.BlockSpec((seq, 2 * LANES), lambda b, p, i: (b, p)),
                  pl.BlockSpec((nq, 2 * V_ROWS, tq), lambda b, p, i: (b, p, 0)),
                  _const_spec(kpad.shape)],
        out_specs=pl.BlockSpec((tq, LANES), lambda b, p, i: (b * nq + i, p)),
        out_shape=jax.ShapeDtypeStruct((tokens, SELF_WIDTH), bf16),
        scratch_shapes=[pltpu.VMEM((seq + tq, 2 * LANES), bf16),
                        pltpu.VMEM((2, 1, tq), f32), pltpu.VMEM((2, V_ROWS, tq), f32),
                        pltpu.VMEM((2, tq, tq), f32), pltpu.VMEM((2, tq, tq), f32),
                        pltpu.VMEM((2, tq, tq), bf16), pltpu.VMEM((2, tq, tq), bf16),
                        pltpu.VMEM((2, 1, tq), f32), pltpu.VMEM((2, 1, tq), f32)],
        compiler_params=_params(("arbitrary", "arbitrary", "arbitrary")),
        name="flash_attention",
    )(qa, ka, vt, kpad)


def _ffn(hs, hm, h, woa, wob, g, wgu, wd, gf, final):
    tokens = h.shape[0]
    tok = lambda t: (t, 0)
    return pl.pallas_call(
        functools.partial(_ffn_kernel, final=final),
        grid=(tokens // FFN_TILE,),
        in_specs=[pl.BlockSpec((FFN_TILE, SELF_WIDTH), tok),
                  pl.BlockSpec((FFN_TILE, MEM_WIDTH), tok),
                  pl.BlockSpec((FFN_TILE, D_MODEL), tok),
                  _const_spec(woa.shape), _const_spec(wob.shape),
                  _const_spec((1, D_MODEL)),
                  _const_spec(wgu.shape), _const_spec(wd.shape),
                  _const_spec((1, D_MODEL))],
        out_specs=pl.BlockSpec((FFN_TILE, D_MODEL), tok),
        out_shape=jax.ShapeDtypeStruct((tokens, D_MODEL), jnp.float32),
        compiler_params=_params(("arbitrary",)),
        name="out_proj_ffn",
    )(hs, hm, h, woa, wob, g, wgu, wd, gf)


def kernel(x, mem, norm_mix, norm_mem, norm_ffn, norm_final, w_in_fox, b_fgate,
           w_in_moba, w_mem_kv, w_out, w_gate_up, w_down):
    batch, seq, _ = x.shape
    depth = w_out.shape[0]
    bf16 = jnp.bfloat16
    assert seq % TOKEN_TILE == 0 and seq // MOBA_BLOCK <= MAX_BLOCKS
    assert (batch * seq) % FFN_TILE == 0

    slopes = tuple(2.0 ** (-8.0 * (hd + 1) / N_SELF_HEADS) for hd in range(N_SELF_HEADS))
    h = x.reshape(batch * seq, D_MODEL)
    mkv_all = _mem_kv(mem.reshape(batch * N_MEM, D_MODEL), norm_mem,
                      w_mem_kv.astype(bf16), batch)
    gf = norm_final.reshape(1, D_MODEL)

    for i in range(depth):
        j = i // 2
        gain = norm_mix[i].reshape(1, D_MODEL)
        if i % 2 == 0:
            w = w_in_fox[j]
            f_cols = jnp.pad(w[:, FOX_F_OFF:FOX_F_OFF + N_SELF_HEADS],
                             ((0, 0), (0, LANES - N_SELF_HEADS)))
            w = jnp.concatenate([w[:, :FOX_F_OFF], f_cols,
                                 w[:, FOX_F_OFF + N_SELF_HEADS:]], axis=1).astype(bf16)
            bias = jnp.pad(b_fgate[j], (0, LANES - N_SELF_HEADS)).reshape(1, LANES)
            qa, ka, vt, hm = _proj_call(
                _fox_proj_kernel, "fox_proj", h, gain, w, w[:, V_OFF:V_OFF + SELF_WIDTH].T,
                [bias], mkv_all[i], batch, seq, [pltpu.VMEM((8, LANES), jnp.float32)],
                q_feature_major=False)
        else:
            w = w_in_moba[j].astype(bf16)
            wqvt = jnp.concatenate([w[:, Q_OFF:Q_OFF + SELF_WIDTH],
                                    w[:, V_OFF:V_OFF + SELF_WIDTH]], axis=1).T
            qa, ka, vt, hm = _proj_call(
                functools.partial(_moba_proj_kernel, slopes=slopes), "moba_proj",
                h, gain, w, wqvt, [], mkv_all[i], batch, seq,
                [pltpu.VMEM((N_PAIRS, 2 * MAX_BLOCKS, LANES), jnp.float32)],
                q_feature_major=True)
        hs = _flash(qa, ka, vt, batch, seq, q_feature_major=(i % 2 == 1))
        wo = w_out[i].astype(bf16)
        h = _ffn(hs, hm, h, wo[:SELF_WIDTH], wo[SELF_WIDTH:],
                 norm_ffn[i].reshape(1, D_MODEL), w_gate_up[i].astype(bf16),
                 w_down[i].astype(bf16), gf, final=(i == depth - 1))
    return h.reshape(batch, seq, D_MODEL)
```

```python
import functools

import jax
import jax.numpy as jnp
from jax import lax
from jax.experimental import pallas as pl
from jax.experimental.pallas import tpu as pltpu

D_MODEL = 1024
N_SELF_HEADS = 12
N_MEM_HEADS = 4
HEAD_DIM = 64
SELF_WIDTH = N_SELF_HEADS * HEAD_DIM
MEM_WIDTH = N_MEM_HEADS * HEAD_DIM
N_MEM = 256
D_FF = 2816
MOBA_BLOCK = 256
MOBA_TOPK = 3
RMS_EPS = 1e-6
NEG = -1e30
BELOW_NEG = -3e38

LANES = 128
N_PAIRS = N_SELF_HEADS // 2
AUG_WIDTH = N_SELF_HEADS * LANES
TOKEN_TILE = MOBA_BLOCK
FFN_TILE = 256
MAX_BLOCKS = 32
SEL_ROW = 16
SCALE = HEAD_DIM ** -0.5
LOG2E = 1.4426950408889634
V_ROWS = 80
VMEM_LIMIT = 56 * 1024 * 1024

Q_OFF, K_OFF, V_OFF = 0, SELF_WIDTH, 2 * SELF_WIDTH
FOX_F_OFF = 3 * SELF_WIDTH
FOX_QM_OFF = FOX_F_OFF + LANES
FOX_PROJ_PAD = FOX_QM_OFF + MEM_WIDTH
MOBA_QM_OFF = 3 * SELF_WIDTH
MOBA_PROJ = MOBA_QM_OFF + MEM_WIDTH


def _nt_dot(a, b):
    return lax.dot_general(a, b, (((1,), (1,)), ((), ())),
                           preferred_element_type=jnp.float32)


def _dot(a, b):
    return jnp.dot(a, b, preferred_element_type=jnp.float32)


def _rms(xf, g):
    ms = jnp.mean(xf * xf, axis=-1, keepdims=True)
    return xf * lax.rsqrt(ms + RMS_EPS) * g


def _split3(x):
    hi = x.astype(jnp.bfloat16).astype(jnp.float32)
    r = x - hi
    mid = r.astype(jnp.bfloat16).astype(jnp.float32)
    return hi, mid, r - mid


def _memory_attention(qm, mk_ref, mv_ref, hm_ref, lane):
    for p in range(N_MEM_HEADS // 2):
        blk = qm[:, LANES * p:LANES * (p + 1)]
        mk = mk_ref[:, LANES * p:LANES * (p + 1)]
        mv = mv_ref[:, LANES * p:LANES * (p + 1)]
        outs = []
        for e in range(2):
            in_head = (lane >= HEAD_DIM * e) & (lane < HEAD_DIM * (e + 1))
            qh = jnp.where(in_head, blk, 0.0).astype(jnp.bfloat16)
            s = _nt_dot(qh, mk) * SCALE
            s = s - jnp.max(s, axis=-1, keepdims=True)
            pr = jnp.exp(s)
            den = jnp.sum(pr, axis=-1, keepdims=True)
            outs.append(_dot(pr.astype(jnp.bfloat16), mv) / den)
        out = jnp.where(lane < HEAD_DIM, outs[0], outs[1])
        hm_ref[:, LANES * p:LANES * (p + 1)] = out.astype(hm_ref.dtype)


def _write_vt(vt_ref, vt):
    tm = vt.shape[1]
    pad_rows = V_ROWS - HEAD_DIM
    ones_blk = jnp.where(lax.broadcasted_iota(jnp.int32, (pad_rows, tm), 0) == 0, 1.0, 0.0)
    for hd in range(N_SELF_HEADS):
        vt_ref[0, V_ROWS * hd:V_ROWS * hd + HEAD_DIM, :] = (
            vt[HEAD_DIM * hd:HEAD_DIM * (hd + 1)].astype(vt_ref.dtype))
        vt_ref[0, V_ROWS * hd + HEAD_DIM:V_ROWS * (hd + 1), :] = ones_blk.astype(vt_ref.dtype)


def _fox_proj_kernel(h_ref, g_ref, w_ref, wqvt_ref, b_ref, mk_ref, mv_ref,
                     qt_ref, ka_ref, vt_ref, hm_ref, carry_ref):
    i = pl.program_id(1)
    tm = h_ref.shape[0]
    xn = _rms(h_ref[...], g_ref[...]).astype(jnp.bfloat16)
    lane = lax.broadcasted_iota(jnp.int32, (tm, LANES), 1)

    qvt = _nt_dot(wqvt_ref[...], xn)
    _write_vt(vt_ref, qvt[SELF_WIDTH:])
    qm = _dot(xn, w_ref[:, FOX_QM_OFF:FOX_QM_OFF + MEM_WIDTH])
    _memory_attention(qm, mk_ref, mv_ref, hm_ref, lane)

    f_logit = _dot(xn, w_ref[:, FOX_F_OFF:FOX_F_OFF + LANES]) + b_ref[...]
    log_f = jnp.minimum(f_logit, 0.0) - jnp.log(1.0 + jnp.exp(-jnp.abs(f_logit)))
    row = lax.broadcasted_iota(jnp.int32, (tm, tm), 0)
    col = lax.broadcasted_iota(jnp.int32, (tm, tm), 1)
    tri = jnp.where(row >= col, 1.0, 0.0).astype(jnp.bfloat16)
    hi, mid, lo = _split3(log_f)
    parts = jnp.concatenate([hi, mid, lo], axis=1).astype(jnp.bfloat16)
    cs = _dot(tri, parts)
    local = cs[:, :LANES] + cs[:, LANES:2 * LANES] + cs[:, 2 * LANES:]

    @pl.when(i == 0)
    def _():
        carry_ref[...] = jnp.zeros_like(carry_ref)

    f_cum = local + carry_ref[0:1, :]
    carry_ref[0:1, :] = f_cum[tm - 1:tm, :]
    f_log2 = f_cum * LOG2E
    f_cols = _split3(f_log2)
    f_rows = _split3(f_log2.T)

    k = _dot(xn, w_ref[:, K_OFF:K_OFF + SELF_WIDTH])
    brow = lax.broadcasted_iota(jnp.int32, (SEL_ROW, tm), 0)
    tail = jnp.zeros((HEAD_DIM - SEL_ROW, tm), jnp.float32)
    for p in range(N_PAIRS):
        kp = k[:, LANES * p:LANES * (p + 1)]
        for e in range(2):
            hd = 2 * p + e
            base = HEAD_DIM * (1 - e)
            bias = jnp.where((brow >= 3) & (brow < 6), 1.0, 0.0)
            ext_k = jnp.where((lane >= base) & (lane < base + 3), 1.0, 0.0)
            for c in range(3):
                bias = jnp.where(brow == c, f_rows[c][hd:hd + 1], bias)
                ext_k = jnp.where(lane == base + 3 + c, -f_cols[c][:, hd:hd + 1], ext_k)
            data = qvt[HEAD_DIM * hd:HEAD_DIM * (hd + 1)] * (SCALE * LOG2E)
            parts = [data, bias, tail] if e == 0 else [bias, tail, data]
            qt_ref[0, LANES * hd:LANES * (hd + 1), :] = (
                jnp.concatenate(parts, axis=0).astype(qt_ref.dtype))
            in_head = (lane >= HEAD_DIM * e) & (lane < HEAD_DIM * (e + 1))
            ka_ref[:, LANES * hd:LANES * (hd + 1)] = (
                jnp.where(in_head, kp, ext_k).astype(ka_ref.dtype))


def _moba_proj_kernel(h_ref, g_ref, w_ref, wqvt_ref, mk_ref, mv_ref,
                      qt_ref, ka_ref, vt_ref, hm_ref, km_ref, *, slopes):
    i = pl.program_id(1)
    tm = h_ref.shape[0]
    xn = _rms(h_ref[...], g_ref[...]).astype(jnp.bfloat16)
    lane = lax.broadcasted_iota(jnp.int32, (tm, LANES), 1)
    lane_row = lax.broadcasted_iota(jnp.int32, (1, LANES), 1)

    qvt = _nt_dot(wqvt_ref[...], xn)
    _write_vt(vt_ref, qvt[SELF_WIDTH:])
    qm = _dot(xn, w_ref[:, MOBA_QM_OFF:MOBA_QM_OFF + MEM_WIDTH])
    _memory_attention(qm, mk_ref, mv_ref, hm_ref, lane)

    @pl.when(i == 0)
    def _():
        km_ref[...] = jnp.zeros_like(km_ref)

    k = _dot(xn, w_ref[:, K_OFF:K_OFF + SELF_WIDTH])
    kmean = jnp.mean(k, axis=0, keepdims=True)

    block_start = (i * tm).astype(jnp.float32)
    offset = lax.broadcasted_iota(jnp.int32, (tm, LANES), 0).astype(jnp.float32)
    ext_k = []
    for e in range(2):
        base = HEAD_DIM * (1 - e)
        x = jnp.where((lane >= base) & (lane < base + 3), 1.0, 0.0)
        x = jnp.where((lane >= base + 4) & (lane < base + 7), block_start, x)
        x = jnp.where((lane >= base + 7) & (lane < base + 10), offset, x)
        ext_k.append(jnp.where(lane == base + SEL_ROW + i, 1.0, x))

    pos_row = (i * tm + lax.broadcasted_iota(jnp.int32, (1, tm), 1)).astype(jnp.float32)
    blk = lax.broadcasted_iota(jnp.int32, (MAX_BLOCKS, tm), 0)
    brow = lax.broadcasted_iota(jnp.int32, (SEL_ROW, tm), 0)
    tail = jnp.zeros((HEAD_DIM - SEL_ROW - MAX_BLOCKS, tm), jnp.float32)

    for p in range(N_PAIRS):
        qt_pair = qvt[LANES * p:LANES * (p + 1)]
        kp = k[:, LANES * p:LANES * (p + 1)]
        q_hi = qt_pair.astype(jnp.bfloat16)
        q_lo = (qt_pair - q_hi.astype(jnp.float32)).astype(jnp.bfloat16)
        km = km_ref[p]
        km_hi = km.astype(jnp.bfloat16)
        km_lo = (km - km_hi.astype(jnp.float32)).astype(jnp.bfloat16)
        r = _dot(jnp.concatenate([km_hi, km_lo], axis=0), q_hi)
        gates = r[:2 * MAX_BLOCKS] + r[2 * MAX_BLOCKS:] + _dot(km_hi, q_lo)
        for e in range(2):
            hd = 2 * p + e
            g = jnp.where(blk < i, gates[MAX_BLOCKS * e:MAX_BLOCKS * (e + 1)], BELOW_NEG)
            chosen = blk == i
            for _ in range(MOBA_TOPK):
                best = jnp.max(g, axis=0, keepdims=True)
                first = jnp.min(jnp.where(g == best, blk, 2 * MAX_BLOCKS),
                                axis=0, keepdims=True)
                hit = blk == first
                chosen = chosen | (hit & (best > 0.5 * BELOW_NEG))
                g = jnp.where(hit, BELOW_NEG, g)
            pen = jnp.where(chosen, 0.0, NEG)

            slope = slopes[hd] * LOG2E
            t_terms = _split3(-slope * pos_row)
            s_terms = _split3(jnp.full((1, tm), slope, jnp.float32))
            bias = jnp.where(brow == 3, 1.0, 0.0)
            for c in range(3):
                bias = jnp.where(brow == c, t_terms[c], bias)
                bias = jnp.where((brow == 4 + c) | (brow == 7 + c), s_terms[c], bias)
            ext_q = jnp.concatenate([bias, pen, tail], axis=0)
            data = qt_pair[HEAD_DIM * e:HEAD_DIM * (e + 1)] * (SCALE * LOG2E)
            parts = [data, ext_q] if e == 0 else [ext_q, data]
            qt_ref[0, LANES * hd:LANES * (hd + 1), :] = (
                jnp.concatenate(parts, axis=0).astype(qt_ref.dtype))

            in_head = (lane >= HEAD_DIM * e) & (lane < HEAD_DIM * (e + 1))
            ka_ref[:, LANES * hd:LANES * (hd + 1)] = (
                jnp.where(in_head, kp, ext_k[e]).astype(ka_ref.dtype))

            in_head_row = (lane_row >= HEAD_DIM * e) & (lane_row < HEAD_DIM * (e + 1))
            km_row = jnp.where(in_head_row, kmean[:, LANES * p:LANES * (p + 1)], 0.0)
            km_ref[p, pl.ds(MAX_BLOCKS * e + i, 1), :] = km_row


def _flash_kernel(qt_ref, ka_ref, vt_ref, kpad_ref, o_ref,
                  kbuf, m_ref, acc_ref, s0, s1, p0, p1, a0, a1):
    i = pl.program_id(2)
    nq = pl.num_programs(2)
    tq = o_ref.shape[0]
    tk = vt_ref.shape[2]
    seq = ka_ref.shape[0]

    @pl.when(i == 0)
    def _():
        kbuf[0:seq, :] = ka_ref[...]
        kbuf[seq:seq + tk, :] = kpad_ref[...]

    def scores(tile, s_ref):
        rows = pl.ds(pl.multiple_of(tile * tk, tk), tk)
        for e in range(2):
            s_ref[e] = _dot(kbuf[rows, LANES * e:LANES * (e + 1)],
                            qt_ref[0, LANES * e:LANES * (e + 1), :])

    def softmax(s_ref, p_ref, a_ref, causal=False):
        for e in range(2):
            st = s_ref[e]
            if causal:
                kv = lax.broadcasted_iota(jnp.int32, (tk, tq), 0)
                qi = lax.broadcasted_iota(jnp.int32, (tk, tq), 1)
                st = jnp.where(kv <= qi, st, NEG)
            m_prev = m_ref[e]
            m_new = jnp.maximum(m_prev, jnp.max(st, axis=0, keepdims=True))
            a_ref[e] = jnp.exp2(m_prev - m_new)
            p_ref[e] = jnp.exp2(st - m_new).astype(p_ref.dtype)
            m_ref[e] = m_new

    def values(tile, p_ref, a_ref):
        for e in range(2):
            vt = vt_ref[tile, V_ROWS * e:V_ROWS * (e + 1), :]
            acc_ref[e] = a_ref[e] * acc_ref[e] + _dot(vt, p_ref[e])

    def key_tile(pos):
        return jnp.where(pos < i, pos, nq)

    def value_tile(pos):
        return jnp.where(pos < 0, i, jnp.minimum(pos, nq - 1))

    m_ref[...] = jnp.full(m_ref.shape, NEG, jnp.float32)
    acc_ref[...] = jnp.zeros(acc_ref.shape, jnp.float32)
    scores(i, s1)
    scores(key_tile(0), s0)
    softmax(s1, p1, a1, causal=True)

    def body(t, carry):
        pos = 2 * t
        scores(key_tile(pos + 1), s1)
        values(value_tile(pos - 1), p1, a1)
        softmax(s0, p0, a0)
        scores(key_tile(pos + 2), s0)
        values(value_tile(pos), p0, a0)
        softmax(s1, p1, a1)
        return carry

    lax.fori_loop(0, (i + 2) // 2, body, 0)

    outs = []
    for e in range(2):
        acc = acc_ref[e]
        outs.append(acc[:HEAD_DIM] / acc[HEAD_DIM:HEAD_DIM + 1])
    o_ref[...] = jnp.concatenate(outs, axis=0).T.astype(o_ref.dtype)


def _ffn_kernel(hs_ref, hm_ref, h_ref, woa_ref, wob_ref, g_ref, wgu_ref, wd_ref,
                gf_ref, o_ref, *, final):
    h1 = h_ref[...] + _dot(hs_ref[...], woa_ref[...]) + _dot(hm_ref[...], wob_ref[...])
    hn = _rms(h1, g_ref[...]).astype(jnp.bfloat16)
    ffn = None
    for c0, c1 in ((0, 1536), (1536, D_FF)):
        gate = _dot(hn, wgu_ref[:, c0:c1])
        up = _dot(hn, wgu_ref[:, D_FF + c0:D_FF + c1])
        act = (gate * jax.nn.sigmoid(gate) * up).astype(jnp.bfloat16)
        down = _dot(act, wd_ref[c0:c1, :])
        ffn = down if ffn is None else ffn + down
    y = h1 + ffn
    if final:
        y = _rms(y, gf_ref[...])
    o_ref[...] = y


def _mem_kv_kernel(mem_ref, g_ref, w_ref, o_ref):
    mn = _rms(mem_ref[...], g_ref[0]).astype(jnp.bfloat16)
    o_ref[0] = _dot(mn, w_ref[0]).astype(o_ref.dtype)


def _params(sem, flags=None):
    return pltpu.CompilerParams(dimension_semantics=sem, vmem_limit_bytes=VMEM_LIMIT,
                                flags=flags)


def _const_spec(shape):
    nd = len(shape)
    return pl.BlockSpec(shape, lambda *_: (0,) * nd)


def _mem_kv(mem2, norm_mem, w_mem_kv, batch):
    depth = w_mem_kv.shape[0]
    return pl.pallas_call(
        _mem_kv_kernel,
        grid=(depth, batch),
        in_specs=[pl.BlockSpec((N_MEM, D_MODEL), lambda d, b: (b, 0)),
                  pl.BlockSpec((1, 1, D_MODEL), lambda d, b: (d, 0, 0)),
                  pl.BlockSpec((1, D_MODEL, 2 * MEM_WIDTH), lambda d, b: (d, 0, 0))],
        out_specs=pl.BlockSpec((1, N_MEM, 2 * MEM_WIDTH), lambda d, b: (d, b, 0)),
        out_shape=jax.ShapeDtypeStruct((depth, batch * N_MEM, 2 * MEM_WIDTH), jnp.bfloat16),
        compiler_params=_params(("arbitrary", "arbitrary")),
        name="mem_kv",
    )(mem2, norm_mem.reshape(depth, 1, D_MODEL), w_mem_kv)


def _proj_call(kernel_fn, name, h, gain, w, extra, mkv, batch, seq, scratch):
    wt = jnp.concatenate([w[:, Q_OFF:Q_OFF + SELF_WIDTH],
                          w[:, V_OFF:V_OFF + SELF_WIDTH]], axis=1).T
    nt = seq // TOKEN_TILE
    tokens = batch * seq
    tok = lambda b, i: (b * nt + i, 0)
    tile = lambda b, i: (b * nt + i, 0, 0)
    bf16 = jnp.bfloat16
    in_specs = [pl.BlockSpec((TOKEN_TILE, D_MODEL), tok),
                _const_spec((1, D_MODEL)),
                _const_spec(w.shape),
                _const_spec(wt.shape)]
    args = [h, gain, w, wt]
    for a in extra:
        in_specs.append(_const_spec(a.shape))
        args.append(a)
    in_specs += [pl.BlockSpec((N_MEM, MEM_WIDTH), lambda b, i: (b, 0)),
                 pl.BlockSpec((N_MEM, MEM_WIDTH), lambda b, i: (b, 1))]
    args += [mkv, mkv]
    out_specs = [pl.BlockSpec((1, AUG_WIDTH, TOKEN_TILE), tile),
                 pl.BlockSpec((TOKEN_TILE, AUG_WIDTH), tok),
                 pl.BlockSpec((1, N_SELF_HEADS * V_ROWS, TOKEN_TILE), tile),
                 pl.BlockSpec((TOKEN_TILE, MEM_WIDTH), tok)]
    out_shape = [jax.ShapeDtypeStruct((tokens // TOKEN_TILE, AUG_WIDTH, TOKEN_TILE), bf16),
                 jax.ShapeDtypeStruct((tokens, AUG_WIDTH), bf16),
                 jax.ShapeDtypeStruct((tokens // TOKEN_TILE, N_SELF_HEADS * V_ROWS, TOKEN_TILE),
                                      bf16),
                 jax.ShapeDtypeStruct((tokens, MEM_WIDTH), bf16)]
    return pl.pallas_call(
        kernel_fn,
        grid=(batch, nt),
        in_specs=in_specs,
        out_specs=out_specs,
        out_shape=out_shape,
        scratch_shapes=scratch,
        compiler_params=_params(("arbitrary", "arbitrary")),
        name=name,
    )(*args)


def _flash(qt, ka, vt, batch, seq):
    nq = seq // TOKEN_TILE
    tokens = batch * seq
    tq = TOKEN_TILE
    f32, bf16 = jnp.float32, jnp.bfloat16
    lane = jnp.arange(2 * LANES)
    ones_lane = (lane == HEAD_DIM + 3) | (lane == LANES + 3)
    kpad = jnp.broadcast_to(jnp.where(ones_lane, NEG, 0.0), (tq, 2 * LANES)).astype(bf16)
    return pl.pallas_call(
        _flash_kernel,
        grid=(batch, N_PAIRS, nq),
        in_specs=[pl.BlockSpec((1, 2 * LANES, tq), lambda b, p, i: (b * nq + i, p, 0)),
                  pl.BlockSpec((seq, 2 * LANES), lambda b, p, i: (b, p)),
                  pl.BlockSpec((nq, 2 * V_ROWS, tq), lambda b, p, i: (b, p, 0)),
                  _const_spec(kpad.shape)],
        out_specs=pl.BlockSpec((tq, LANES), lambda b, p, i: (b * nq + i, p)),
        out_shape=jax.ShapeDtypeStruct((tokens, SELF_WIDTH), bf16),
        scratch_shapes=[pltpu.VMEM((seq + tq, 2 * LANES), bf16),
                        pltpu.VMEM((2, 1, tq), f32), pltpu.VMEM((2, V_ROWS, tq), f32),
                        pltpu.VMEM((2, tq, tq), f32), pltpu.VMEM((2, tq, tq), f32),
                        pltpu.VMEM((2, tq, tq), bf16), pltpu.VMEM((2, tq, tq), bf16),
                        pltpu.VMEM((2, 1, tq), f32), pltpu.VMEM((2, 1, tq), f32)],
        compiler_params=_params(("arbitrary", "arbitrary", "arbitrary")),
        name="flash_attention",
    )(qt, ka, vt, kpad)


def _ffn(hs, hm, h, woa, wob, g, wgu, wd, gf, final):
    tokens = h.shape[0]
    tok = lambda t: (t, 0)
    return pl.pallas_call(
        functools.partial(_ffn_kernel, final=final),
        grid=(tokens // FFN_TILE,),
        in_specs=[pl.BlockSpec((FFN_TILE, SELF_WIDTH), tok),
                  pl.BlockSpec((FFN_TILE, MEM_WIDTH), tok),
                  pl.BlockSpec((FFN_TILE, D_MODEL), tok),
                  _const_spec(woa.shape), _const_spec(wob.shape),
                  _const_spec((1, D_MODEL)),
                  _const_spec(wgu.shape), _const_spec(wd.shape),
                  _const_spec((1, D_MODEL))],
        out_specs=pl.BlockSpec((FFN_TILE, D_MODEL), tok),
        out_shape=jax.ShapeDtypeStruct((tokens, D_MODEL), jnp.float32),
        compiler_params=_params(("arbitrary",)),
        name="out_proj_ffn",
    )(hs, hm, h, woa, wob, g, wgu, wd, gf)


def kernel(x, mem, norm_mix, norm_mem, norm_ffn, norm_final, w_in_fox, b_fgate,
           w_in_moba, w_mem_kv, w_out, w_gate_up, w_down):
    batch, seq, _ = x.shape
    depth = w_out.shape[0]
    bf16 = jnp.bfloat16
    assert seq % TOKEN_TILE == 0 and seq // MOBA_BLOCK <= MAX_BLOCKS
    assert (batch * seq) % FFN_TILE == 0

    slopes = tuple(2.0 ** (-8.0 * (hd + 1) / N_SELF_HEADS) for hd in range(N_SELF_HEADS))
    h = x.reshape(batch * seq, D_MODEL)
    mkv_all = _mem_kv(mem.reshape(batch * N_MEM, D_MODEL), norm_mem,
                      w_mem_kv.astype(bf16), batch)
    gf = norm_final.reshape(1, D_MODEL)

    for i in range(depth):
        j = i // 2
        gain = norm_mix[i].reshape(1, D_MODEL)
        if i % 2 == 0:
            w = w_in_fox[j]
            f_cols = jnp.pad(w[:, FOX_F_OFF:FOX_F_OFF + N_SELF_HEADS],
                             ((0, 0), (0, LANES - N_SELF_HEADS)))
            w = jnp.concatenate([w[:, :FOX_F_OFF], f_cols,
                                 w[:, FOX_F_OFF + N_SELF_HEADS:]], axis=1).astype(bf16)
            bias = jnp.pad(b_fgate[j], (0, LANES - N_SELF_HEADS)).reshape(1, LANES)
            qt, ka, vt, hm = _proj_call(
                _fox_proj_kernel, "fox_proj", h, gain, w, [bias], mkv_all[i], batch, seq,
                [pltpu.VMEM((8, LANES), jnp.float32)])
        else:
            qt, ka, vt, hm = _proj_call(
                functools.partial(_moba_proj_kernel, slopes=slopes), "moba_proj",
                h, gain, w_in_moba[j].astype(bf16), [], mkv_all[i], batch, seq,
                [pltpu.VMEM((N_PAIRS, 2 * MAX_BLOCKS, LANES), jnp.float32)])
        hs = _flash(qt, ka, vt, batch, seq)
        wo = w_out[i].astype(bf16)
        h = _ffn(hs, hm, h, wo[:SELF_WIDTH], wo[SELF_WIDTH:],
                 norm_ffn[i].reshape(1, D_MODEL), w_gate_up[i].astype(bf16),
                 w_down[i].astype(bf16), gf, final=(i == depth - 1))
    return h.reshape(batch, seq, D_MODEL)
```

```python
import functools

import jax
import jax.numpy as jnp
from jax import lax
from jax.experimental import pallas as pl
from jax.experimental.pallas import tpu as pltpu

D_MODEL = 1024
N_SELF_HEADS = 12
N_MEM_HEADS = 4
HEAD_DIM = 64
SELF_WIDTH = N_SELF_HEADS * HEAD_DIM
MEM_WIDTH = N_MEM_HEADS * HEAD_DIM
N_MEM = 256
D_FF = 2816
MOBA_BLOCK = 256
MOBA_TOPK = 3
RMS_EPS = 1e-6
NEG = -1e30
BELOW_NEG = -3e38

LANES = 128
N_PAIRS = N_SELF_HEADS // 2
AUG_WIDTH = N_SELF_HEADS * LANES
TOKEN_TILE = MOBA_BLOCK
FFN_TILE = 256
FLASH_UNROLL = 4
MAX_BLOCKS = 32
SEL_ROW = 16
SCALE = HEAD_DIM ** -0.5
LOG2E = 1.4426950408889634
V_ROWS = 80
VMEM_LIMIT = 56 * 1024 * 1024

Q_OFF, K_OFF, V_OFF = 0, SELF_WIDTH, 2 * SELF_WIDTH
FOX_F_OFF = 3 * SELF_WIDTH
FOX_QM_OFF = FOX_F_OFF + LANES
FOX_PROJ_PAD = FOX_QM_OFF + MEM_WIDTH
MOBA_QM_OFF = 3 * SELF_WIDTH
MOBA_PROJ = MOBA_QM_OFF + MEM_WIDTH


def _nt_dot(a, b):
    return lax.dot_general(a, b, (((1,), (1,)), ((), ())),
                           preferred_element_type=jnp.float32)


def _dot(a, b):
    return jnp.dot(a, b, preferred_element_type=jnp.float32)


def _rms(xf, g):
    ms = jnp.mean(xf * xf, axis=-1, keepdims=True)
    return xf * lax.rsqrt(ms + RMS_EPS) * g


def _split3(x):
    hi = x.astype(jnp.bfloat16).astype(jnp.float32)
    r = x - hi
    mid = r.astype(jnp.bfloat16).astype(jnp.float32)
    return hi, mid, r - mid


def _memory_attention(qm, mk_ref, mv_ref, hm_ref, lane):
    for p in range(N_MEM_HEADS // 2):
        blk = qm[:, LANES * p:LANES * (p + 1)]
        mk = mk_ref[:, LANES * p:LANES * (p + 1)]
        mv = mv_ref[:, LANES * p:LANES * (p + 1)]
        outs = []
        for e in range(2):
            in_head = (lane >= HEAD_DIM * e) & (lane < HEAD_DIM * (e + 1))
            qh = jnp.where(in_head, blk, 0.0).astype(jnp.bfloat16)
            s = _nt_dot(qh, mk) * SCALE
            s = s - jnp.max(s, axis=-1, keepdims=True)
            pr = jnp.exp(s)
            den = jnp.sum(pr, axis=-1, keepdims=True)
            outs.append(_dot(pr.astype(jnp.bfloat16), mv) / den)
        out = jnp.where(lane < HEAD_DIM, outs[0], outs[1])
        hm_ref[:, LANES * p:LANES * (p + 1)] = out.astype(hm_ref.dtype)


def _write_vt(vt_ref, vt):
    tm = vt.shape[1]
    pad_rows = V_ROWS - HEAD_DIM
    ones_blk = jnp.where(lax.broadcasted_iota(jnp.int32, (pad_rows, tm), 0) == 0, 1.0, 0.0)
    for hd in range(N_SELF_HEADS):
        vt_ref[0, V_ROWS * hd:V_ROWS * hd + HEAD_DIM, :] = (
            vt[HEAD_DIM * hd:HEAD_DIM * (hd + 1)].astype(vt_ref.dtype))
        vt_ref[0, V_ROWS * hd + HEAD_DIM:V_ROWS * (hd + 1), :] = ones_blk.astype(vt_ref.dtype)


def _fox_proj_kernel(h_ref, g_ref, w_ref, wqvt_ref, b_ref, mk_ref, mv_ref,
                     qt_ref, ka_ref, vt_ref, hm_ref, carry_ref):
    i = pl.program_id(1)
    tm = h_ref.shape[0]
    xn = _rms(h_ref[...], g_ref[...]).astype(jnp.bfloat16)
    lane = lax.broadcasted_iota(jnp.int32, (tm, LANES), 1)

    qvt = _nt_dot(wqvt_ref[...], xn)
    _write_vt(vt_ref, qvt[SELF_WIDTH:])
    qm = _dot(xn, w_ref[:, FOX_QM_OFF:FOX_QM_OFF + MEM_WIDTH])
    _memory_attention(qm, mk_ref, mv_ref, hm_ref, lane)

    f_logit = _dot(xn, w_ref[:, FOX_F_OFF:FOX_F_OFF + LANES]) + b_ref[...]
    log_f = jnp.minimum(f_logit, 0.0) - jnp.log(1.0 + jnp.exp(-jnp.abs(f_logit)))
    row = lax.broadcasted_iota(jnp.int32, (tm, tm), 0)
    col = lax.broadcasted_iota(jnp.int32, (tm, tm), 1)
    tri = jnp.where(row >= col, 1.0, 0.0).astype(jnp.bfloat16)
    hi, mid, lo = _split3(log_f)
    parts = jnp.concatenate([hi, mid, lo], axis=1).astype(jnp.bfloat16)
    cs = _dot(tri, parts)
    local = cs[:, :LANES] + cs[:, LANES:2 * LANES] + cs[:, 2 * LANES:]

    @pl.when(i == 0)
    def _():
        carry_ref[...] = jnp.zeros_like(carry_ref)

    f_cum = local + carry_ref[0:1, :]
    carry_ref[0:1, :] = f_cum[tm - 1:tm, :]
    f_log2 = f_cum * LOG2E
    f_cols = _split3(f_log2)
    f_rows = _split3(f_log2.T)

    k = _dot(xn, w_ref[:, K_OFF:K_OFF + SELF_WIDTH])
    brow = lax.broadcasted_iota(jnp.int32, (SEL_ROW, tm), 0)
    tail = jnp.zeros((HEAD_DIM - SEL_ROW, tm), jnp.float32)
    for p in range(N_PAIRS):
        kp = k[:, LANES * p:LANES * (p + 1)]
        for e in range(2):
            hd = 2 * p + e
            base = HEAD_DIM * (1 - e)
            bias = jnp.where((brow >= 3) & (brow < 6), 1.0, 0.0)
            ext_k = jnp.where((lane >= base) & (lane < base + 3), 1.0, 0.0)
            for c in range(3):
                bias = jnp.where(brow == c, f_rows[c][hd:hd + 1], bias)
                ext_k = jnp.where(lane == base + 3 + c, -f_cols[c][:, hd:hd + 1], ext_k)
            data = qvt[HEAD_DIM * hd:HEAD_DIM * (hd + 1)] * (SCALE * LOG2E)
            parts = [data, bias, tail] if e == 0 else [bias, tail, data]
            qt_ref[0, LANES * hd:LANES * (hd + 1), :] = (
                jnp.concatenate(parts, axis=0).astype(qt_ref.dtype))
            in_head = (lane >= HEAD_DIM * e) & (lane < HEAD_DIM * (e + 1))
            ka_ref[:, LANES * hd:LANES * (hd + 1)] = (
                jnp.where(in_head, kp, ext_k).astype(ka_ref.dtype))


def _moba_proj_kernel(h_ref, g_ref, w_ref, wqvt_ref, mk_ref, mv_ref,
                      qt_ref, ka_ref, vt_ref, hm_ref, km_ref, *, slopes):
    i = pl.program_id(1)
    tm = h_ref.shape[0]
    xn = _rms(h_ref[...], g_ref[...]).astype(jnp.bfloat16)
    lane = lax.broadcasted_iota(jnp.int32, (tm, LANES), 1)
    lane_row = lax.broadcasted_iota(jnp.int32, (1, LANES), 1)

    qvt = _nt_dot(wqvt_ref[...], xn)
    _write_vt(vt_ref, qvt[SELF_WIDTH:])
    qm = _dot(xn, w_ref[:, MOBA_QM_OFF:MOBA_QM_OFF + MEM_WIDTH])
    _memory_attention(qm, mk_ref, mv_ref, hm_ref, lane)

    @pl.when(i == 0)
    def _():
        km_ref[...] = jnp.zeros_like(km_ref)

    k = _dot(xn, w_ref[:, K_OFF:K_OFF + SELF_WIDTH])
    kmean = jnp.mean(k, axis=0, keepdims=True)

    block_start = (i * tm).astype(jnp.float32)
    offset = lax.broadcasted_iota(jnp.int32, (tm, LANES), 0).astype(jnp.float32)
    ext_k = []
    for e in range(2):
        base = HEAD_DIM * (1 - e)
        x = jnp.where((lane >= base) & (lane < base + 3), 1.0, 0.0)
        x = jnp.where((lane >= base + 4) & (lane < base + 7), block_start, x)
        x = jnp.where((lane >= base + 7) & (lane < base + 10), offset, x)
        ext_k.append(jnp.where(lane == base + SEL_ROW + i, 1.0, x))

    pos_row = (i * tm + lax.broadcasted_iota(jnp.int32, (1, tm), 1)).astype(jnp.float32)
    blk = lax.broadcasted_iota(jnp.int32, (MAX_BLOCKS, tm), 0)
    brow = lax.broadcasted_iota(jnp.int32, (SEL_ROW, tm), 0)
    tail = jnp.zeros((HEAD_DIM - SEL_ROW - MAX_BLOCKS, tm), jnp.float32)

    for p in range(N_PAIRS):
        qt_pair = qvt[LANES * p:LANES * (p + 1)]
        kp = k[:, LANES * p:LANES * (p + 1)]
        q_hi = qt_pair.astype(jnp.bfloat16)
        q_lo = (qt_pair - q_hi.astype(jnp.float32)).astype(jnp.bfloat16)
        km = km_ref[p]
        km_hi = km.astype(jnp.bfloat16)
        km_lo = (km - km_hi.astype(jnp.float32)).astype(jnp.bfloat16)
        r = _dot(jnp.concatenate([km_hi, km_lo], axis=0), q_hi)
        gates = r[:2 * MAX_BLOCKS] + r[2 * MAX_BLOCKS:] + _dot(km_hi, q_lo)
        for e in range(2):
            hd = 2 * p + e
            g = jnp.where(blk < i, gates[MAX_BLOCKS * e:MAX_BLOCKS * (e + 1)], BELOW_NEG)
            chosen = blk == i
            for _ in range(MOBA_TOPK):
                best = jnp.max(g, axis=0, keepdims=True)
                first = jnp.min(jnp.where(g == best, blk, 2 * MAX_BLOCKS),
                                axis=0, keepdims=True)
                hit = blk == first
                chosen = chosen | (hit & (best > 0.5 * BELOW_NEG))
                g = jnp.where(hit, BELOW_NEG, g)
            pen = jnp.where(chosen, 0.0, NEG)

            slope = slopes[hd] * LOG2E
            t_terms = _split3(-slope * pos_row)
            s_terms = _split3(jnp.full((1, tm), slope, jnp.float32))
            bias = jnp.where(brow == 3, 1.0, 0.0)
            for c in range(3):
                bias = jnp.where(brow == c, t_terms[c], bias)
                bias = jnp.where((brow == 4 + c) | (brow == 7 + c), s_terms[c], bias)
            ext_q = jnp.concatenate([bias, pen, tail], axis=0)
            data = qt_pair[HEAD_DIM * e:HEAD_DIM * (e + 1)] * (SCALE * LOG2E)
            parts = [data, ext_q] if e == 0 else [ext_q, data]
            qt_ref[0, LANES * hd:LANES * (hd + 1), :] = (
                jnp.concatenate(parts, axis=0).astype(qt_ref.dtype))

            in_head = (lane >= HEAD_DIM * e) & (lane < HEAD_DIM * (e + 1))
            ka_ref[:, LANES * hd:LANES * (hd + 1)] = (
                jnp.where(in_head, kp, ext_k[e]).astype(ka_ref.dtype))

            in_head_row = (lane_row >= HEAD_DIM * e) & (lane_row < HEAD_DIM * (e + 1))
            km_row = jnp.where(in_head_row, kmean[:, LANES * p:LANES * (p + 1)], 0.0)
            km_ref[p, pl.ds(MAX_BLOCKS * e + i, 1), :] = km_row


def _flash_kernel(qt_ref, ka_ref, vt_ref, kpad_ref, o_ref,
                  kbuf, m_ref, acc_ref, s0, s1, p0, p1, a0, a1):
    i = pl.program_id(2)
    nq = pl.num_programs(2)
    tq = o_ref.shape[0]
    tk = vt_ref.shape[2]
    seq = ka_ref.shape[0]

    @pl.when(i == 0)
    def _():
        kbuf[0:seq, :] = ka_ref[...]
        kbuf[seq:seq + tk, :] = kpad_ref[...]

    def scores(tile, s_ref):
        rows = pl.ds(pl.multiple_of(tile * tk, tk), tk)
        for e in range(2):
            s_ref[e] = _dot(kbuf[rows, LANES * e:LANES * (e + 1)],
                            qt_ref[0, LANES * e:LANES * (e + 1), :])

    def softmax(s_ref, p_ref, a_ref, causal=False):
        for e in range(2):
            st = s_ref[e]
            if causal:
                kv = lax.broadcasted_iota(jnp.int32, (tk, tq), 0)
                qi = lax.broadcasted_iota(jnp.int32, (tk, tq), 1)
                st = jnp.where(kv <= qi, st, NEG)
            m_prev = m_ref[e]
            m_new = jnp.maximum(m_prev, jnp.max(st, axis=0, keepdims=True))
            a_ref[e] = jnp.exp2(m_prev - m_new)
            p_ref[e] = jnp.exp2(st - m_new).astype(p_ref.dtype)
            m_ref[e] = m_new

    def values(tile, p_ref, a_ref):
        for e in range(2):
            vt = vt_ref[tile, V_ROWS * e:V_ROWS * (e + 1), :]
            acc_ref[e] = a_ref[e] * acc_ref[e] + _dot(vt, p_ref[e])

    def key_tile(pos):
        return jnp.where(pos < i, pos, nq)

    def value_tile(pos):
        return jnp.where(pos < 0, i, jnp.minimum(pos, nq - 1))

    def pipeline_step(h):
        pos = 2 * h
        scores(key_tile(pos + 1), s1)
        values(value_tile(pos - 1), p1, a1)
        softmax(s0, p0, a0)
        scores(key_tile(pos + 2), s0)
        values(value_tile(pos), p0, a0)
        softmax(s1, p1, a1)

    m_ref[...] = jnp.full(m_ref.shape, NEG, jnp.float32)
    acc_ref[...] = jnp.zeros(acc_ref.shape, jnp.float32)
    scores(i, s1)
    scores(key_tile(0), s0)
    softmax(s1, p1, a1, causal=True)
    pipeline_step(0)

    rest = i // 2
    n_unrolled = rest // FLASH_UNROLL

    def unrolled(t, carry):
        for u in range(FLASH_UNROLL):
            pipeline_step(1 + FLASH_UNROLL * t + u)
        return carry

    def single(t, carry):
        pipeline_step(1 + FLASH_UNROLL * n_unrolled + t)
        return carry

    lax.fori_loop(0, n_unrolled, unrolled, 0)
    lax.fori_loop(0, rest - FLASH_UNROLL * n_unrolled, single, 0)

    outs = []
    for e in range(2):
        acc = acc_ref[e]
        outs.append(acc[:HEAD_DIM] / acc[HEAD_DIM:HEAD_DIM + 1])
    o_ref[...] = jnp.concatenate(outs, axis=0).T.astype(o_ref.dtype)


def _ffn_kernel(hs_ref, hm_ref, h_ref, woa_ref, wob_ref, g_ref, wgu_ref, wd_ref,
                gf_ref, o_ref, *, final):
    h1 = h_ref[...] + _dot(hs_ref[...], woa_ref[...]) + _dot(hm_ref[...], wob_ref[...])
    hn = _rms(h1, g_ref[...]).astype(jnp.bfloat16)
    ffn = None
    for c0, c1 in ((0, 1536), (1536, D_FF)):
        gate = _dot(hn, wgu_ref[:, c0:c1])
        up = _dot(hn, wgu_ref[:, D_FF + c0:D_FF + c1])
        act = (gate * jax.nn.sigmoid(gate) * up).astype(jnp.bfloat16)
        down = _dot(act, wd_ref[c0:c1, :])
        ffn = down if ffn is None else ffn + down
    y = h1 + ffn
    if final:
        y = _rms(y, gf_ref[...])
    o_ref[...] = y


def _mem_kv_kernel(mem_ref, g_ref, w_ref, o_ref):
    mn = _rms(mem_ref[...], g_ref[0]).astype(jnp.bfloat16)
    o_ref[0] = _dot(mn, w_ref[0]).astype(o_ref.dtype)


def _params(sem, flags=None):
    return pltpu.CompilerParams(dimension_semantics=sem, vmem_limit_bytes=VMEM_LIMIT,
                                flags=flags)


def _const_spec(shape):
    nd = len(shape)
    return pl.BlockSpec(shape, lambda *_: (0,) * nd)


def _mem_kv(mem2, norm_mem, w_mem_kv, batch):
    depth = w_mem_kv.shape[0]
    return pl.pallas_call(
        _mem_kv_kernel,
        grid=(depth, batch),
        in_specs=[pl.BlockSpec((N_MEM, D_MODEL), lambda d, b: (b, 0)),
                  pl.BlockSpec((1, 1, D_MODEL), lambda d, b: (d, 0, 0)),
                  pl.BlockSpec((1, D_MODEL, 2 * MEM_WIDTH), lambda d, b: (d, 0, 0))],
        out_specs=pl.BlockSpec((1, N_MEM, 2 * MEM_WIDTH), lambda d, b: (d, b, 0)),
        out_shape=jax.ShapeDtypeStruct((depth, batch * N_MEM, 2 * MEM_WIDTH), jnp.bfloat16),
        compiler_params=_params(("arbitrary", "arbitrary")),
        name="mem_kv",
    )(mem2, norm_mem.reshape(depth, 1, D_MODEL), w_mem_kv)


def _proj_call(kernel_fn, name, h, gain, w, extra, mkv, batch, seq, scratch):
    wt = jnp.concatenate([w[:, Q_OFF:Q_OFF + SELF_WIDTH],
                          w[:, V_OFF:V_OFF + SELF_WIDTH]], axis=1).T
    nt = seq // TOKEN_TILE
    tokens = batch * seq
    tok = lambda b, i: (b * nt + i, 0)
    tile = lambda b, i: (b * nt + i, 0, 0)
    bf16 = jnp.bfloat16
    in_specs = [pl.BlockSpec((TOKEN_TILE, D_MODEL), tok),
                _const_spec((1, D_MODEL)),
                _const_spec(w.shape),
                _const_spec(wt.shape)]
    args = [h, gain, w, wt]
    for a in extra:
        in_specs.append(_const_spec(a.shape))
        args.append(a)
    in_specs += [pl.BlockSpec((N_MEM, MEM_WIDTH), lambda b, i: (b, 0)),
                 pl.BlockSpec((N_MEM, MEM_WIDTH), lambda b, i: (b, 1))]
    args += [mkv, mkv]
    out_specs = [pl.BlockSpec((1, AUG_WIDTH, TOKEN_TILE), tile),
                 pl.BlockSpec((TOKEN_TILE, AUG_WIDTH), tok),
                 pl.BlockSpec((1, N_SELF_HEADS * V_ROWS, TOKEN_TILE), tile),
                 pl.BlockSpec((TOKEN_TILE, MEM_WIDTH), tok)]
    out_shape = [jax.ShapeDtypeStruct((tokens // TOKEN_TILE, AUG_WIDTH, TOKEN_TILE), bf16),
                 jax.ShapeDtypeStruct((tokens, AUG_WIDTH), bf16),
                 jax.ShapeDtypeStruct((tokens // TOKEN_TILE, N_SELF_HEADS * V_ROWS, TOKEN_TILE),
                                      bf16),
                 jax.ShapeDtypeStruct((tokens, MEM_WIDTH), bf16)]
    return pl.pallas_call(
        kernel_fn,
        grid=(batch, nt),
        in_specs=in_specs,
        out_specs=out_specs,
        out_shape=out_shape,
        scratch_shapes=scratch,
        compiler_params=_params(("arbitrary", "arbitrary")),
        name=name,
    )(*args)


def _flash(qt, ka, vt, batch, seq):
    nq = seq // TOKEN_TILE
    tokens = batch * seq
    tq = TOKEN_TILE
    f32, bf16 = jnp.float32, jnp.bfloat16
    lane = jnp.arange(2 * LANES)
    ones_lane = (lane == HEAD_DIM + 3) | (lane == LANES + 3)
    kpad = jnp.broadcast_to(jnp.where(ones_lane, NEG, 0.0), (tq, 2 * LANES)).astype(bf16)
    return pl.pallas_call(
        _flash_kernel,
        grid=(batch, N_PAIRS, nq),
        in_specs=[pl.BlockSpec((1, 2 * LANES, tq), lambda b, p, i: (b * nq + i, p, 0)),
                  pl.BlockSpec((seq, 2 * LANES), lambda b, p, i: (b, p)),
                  pl.BlockSpec((nq, 2 * V_ROWS, tq), lambda b, p, i: (b, p, 0)),
                  _const_spec(kpad.shape)],
        out_specs=pl.BlockSpec((tq, LANES), lambda b, p, i: (b * nq + i, p)),
        out_shape=jax.ShapeDtypeStruct((tokens, SELF_WIDTH), bf16),
        scratch_shapes=[pltpu.VMEM((seq + tq, 2 * LANES), bf16),
                        pltpu.VMEM((2, 1, tq), f32), pltpu.VMEM((2, V_ROWS, tq), f32),
                        pltpu.VMEM((2, tq, tq), f32), pltpu.VMEM((2, tq, tq), f32),
                        pltpu.VMEM((2, tq, tq), bf16), pltpu.VMEM((2, tq, tq), bf16),
                        pltpu.VMEM((2, 1, tq), f32), pltpu.VMEM((2, 1, tq), f32)],
        compiler_params=_params(("arbitrary", "arbitrary", "arbitrary")),
        name="flash_attention",
    )(qt, ka, vt, kpad)


def _ffn(hs, hm, h, woa, wob, g, wgu, wd, gf, final):
    tokens = h.shape[0]
    tok = lambda t: (t, 0)
    return pl.pallas_call(
        functools.partial(_ffn_kernel, final=final),
        grid=(tokens // FFN_TILE,),
        in_specs=[pl.BlockSpec((FFN_TILE, SELF_WIDTH), tok),
                  pl.BlockSpec((FFN_TILE, MEM_WIDTH), tok),
                  pl.BlockSpec((FFN_TILE, D_MODEL), tok),
                  _const_spec(woa.shape), _const_spec(wob.shape),
                  _const_spec((1, D_MODEL)),
                  _const_spec(wgu.shape), _const_spec(wd.shape),
                  _const_spec((1, D_MODEL))],
        out_specs=pl.BlockSpec((FFN_TILE, D_MODEL), tok),
        out_shape=jax.ShapeDtypeStruct((tokens, D_MODEL), jnp.float32),
        compiler_params=_params(("arbitrary",)),
        name="out_proj_ffn",
    )(hs, hm, h, woa, wob, g, wgu, wd, gf)


def kernel(x, mem, norm_mix, norm_mem, norm_ffn, norm_final, w_in_fox, b_fgate,
           w_in_moba, w_mem_kv, w_out, w_gate_up, w_down):
    batch, seq, _ = x.shape
    depth = w_out.shape[0]
    bf16 = jnp.bfloat16
    assert seq % TOKEN_TILE == 0 and seq // MOBA_BLOCK <= MAX_BLOCKS
    assert (batch * seq) % FFN_TILE == 0

    slopes = tuple(2.0 ** (-8.0 * (hd + 1) / N_SELF_HEADS) for hd in range(N_SELF_HEADS))
    h = x.reshape(batch * seq, D_MODEL)
    mkv_all = _mem_kv(mem.reshape(batch * N_MEM, D_MODEL), norm_mem,
                      w_mem_kv.astype(bf16), batch)
    gf = norm_final.reshape(1, D_MODEL)

    for i in range(depth):
        j = i // 2
        gain = norm_mix[i].reshape(1, D_MODEL)
        if i % 2 == 0:
            w = w_in_fox[j]
            f_cols = jnp.pad(w[:, FOX_F_OFF:FOX_F_OFF + N_SELF_HEADS],
                             ((0, 0), (0, LANES - N_SELF_HEADS)))
            w = jnp.concatenate([w[:, :FOX_F_OFF], f_cols,
                                 w[:, FOX_F_OFF + N_SELF_HEADS:]], axis=1).astype(bf16)
            bias = jnp.pad(b_fgate[j], (0, LANES - N_SELF_HEADS)).reshape(1, LANES)
            qt, ka, vt, hm = _proj_call(
                _fox_proj_kernel, "fox_proj", h, gain, w, [bias], mkv_all[i], batch, seq,
                [pltpu.VMEM((8, LANES), jnp.float32)])
        else:
            qt, ka, vt, hm = _proj_call(
                functools.partial(_moba_proj_kernel, slopes=slopes), "moba_proj",
                h, gain, w_in_moba[j].astype(bf16), [], mkv_all[i], batch, seq,
                [pltpu.VMEM((N_PAIRS, 2 * MAX_BLOCKS, LANES), jnp.float32)])
        hs = _flash(qt, ka, vt, batch, seq)
        wo = w_out[i].astype(bf16)
        h = _ffn(hs, hm, h, wo[:SELF_WIDTH], wo[SELF_WIDTH:],
                 norm_ffn[i].reshape(1, D_MODEL), w_gate_up[i].astype(bf16),
                 w_down[i].astype(bf16), gf, final=(i == depth - 1))
    return h.reshape(batch, seq, D_MODEL)
```

```python
import functools

import jax
import jax.numpy as jnp
from jax import lax
from jax.experimental import pallas as pl
from jax.experimental.pallas import tpu as pltpu

D_MODEL = 1024
N_SELF_HEADS = 12
N_MEM_HEADS = 4
HEAD_DIM = 64
SELF_WIDTH = N_SELF_HEADS * HEAD_DIM
MEM_WIDTH = N_MEM_HEADS * HEAD_DIM
N_MEM = 256
D_FF = 2816
MOBA_BLOCK = 256
MOBA_TOPK = 3
RMS_EPS = 1e-6
NEG = -1e30
BELOW_NEG = -3e38

LANES = 128
N_PAIRS = N_SELF_HEADS // 2
AUG_WIDTH = N_SELF_HEADS * LANES
TOKEN_TILE = MOBA_BLOCK
FFN_TILE = 256
FLASH_UNROLL = 4
FLASH_QUERY_TILES = 2
MAX_BLOCKS = 32
SEL_ROW = 16
SCALE = HEAD_DIM ** -0.5
LOG2E = 1.4426950408889634
V_ROWS = 80
VMEM_LIMIT = 56 * 1024 * 1024

Q_OFF, K_OFF, V_OFF = 0, SELF_WIDTH, 2 * SELF_WIDTH
FOX_F_OFF = 3 * SELF_WIDTH
FOX_QM_OFF = FOX_F_OFF + LANES
FOX_PROJ_PAD = FOX_QM_OFF + MEM_WIDTH
MOBA_QM_OFF = 3 * SELF_WIDTH
MOBA_PROJ = MOBA_QM_OFF + MEM_WIDTH


def _nt_dot(a, b):
    return lax.dot_general(a, b, (((1,), (1,)), ((), ())),
                           preferred_element_type=jnp.float32)


def _dot(a, b):
    return jnp.dot(a, b, preferred_element_type=jnp.float32)


def _rms(xf, g):
    ms = jnp.mean(xf * xf, axis=-1, keepdims=True)
    return xf * lax.rsqrt(ms + RMS_EPS) * g


def _split3(x):
    hi = x.astype(jnp.bfloat16).astype(jnp.float32)
    r = x - hi
    mid = r.astype(jnp.bfloat16).astype(jnp.float32)
    return hi, mid, r - mid


def _memory_attention(qm, mk_ref, mv_ref, hm_ref, lane):
    for p in range(N_MEM_HEADS // 2):
        blk = qm[:, LANES * p:LANES * (p + 1)]
        mk = mk_ref[:, LANES * p:LANES * (p + 1)]
        mv = mv_ref[:, LANES * p:LANES * (p + 1)]
        outs = []
        for e in range(2):
            in_head = (lane >= HEAD_DIM * e) & (lane < HEAD_DIM * (e + 1))
            qh = jnp.where(in_head, blk, 0.0).astype(jnp.bfloat16)
            s = _nt_dot(qh, mk) * SCALE
            s = s - jnp.max(s, axis=-1, keepdims=True)
            pr = jnp.exp(s)
            den = jnp.sum(pr, axis=-1, keepdims=True)
            outs.append(_dot(pr.astype(jnp.bfloat16), mv) / den)
        out = jnp.where(lane < HEAD_DIM, outs[0], outs[1])
        hm_ref[:, LANES * p:LANES * (p + 1)] = out.astype(hm_ref.dtype)


def _write_vt(vt_ref, vt):
    tm = vt.shape[1]
    pad_rows = V_ROWS - HEAD_DIM
    ones_blk = jnp.where(lax.broadcasted_iota(jnp.int32, (pad_rows, tm), 0) == 0, 1.0, 0.0)
    for hd in range(N_SELF_HEADS):
        vt_ref[0, V_ROWS * hd:V_ROWS * hd + HEAD_DIM, :] = (
            vt[HEAD_DIM * hd:HEAD_DIM * (hd + 1)].astype(vt_ref.dtype))
        vt_ref[0, V_ROWS * hd + HEAD_DIM:V_ROWS * (hd + 1), :] = ones_blk.astype(vt_ref.dtype)


def _fox_proj_kernel(h_ref, g_ref, w_ref, wqvt_ref, b_ref, mk_ref, mv_ref,
                     qt_ref, ka_ref, vt_ref, hm_ref, carry_ref):
    i = pl.program_id(1)
    tm = h_ref.shape[0]
    xn = _rms(h_ref[...], g_ref[...]).astype(jnp.bfloat16)
    lane = lax.broadcasted_iota(jnp.int32, (tm, LANES), 1)

    qvt = _nt_dot(wqvt_ref[...], xn)
    _write_vt(vt_ref, qvt[SELF_WIDTH:])
    qm = _dot(xn, w_ref[:, FOX_QM_OFF:FOX_QM_OFF + MEM_WIDTH])
    _memory_attention(qm, mk_ref, mv_ref, hm_ref, lane)

    f_logit = _dot(xn, w_ref[:, FOX_F_OFF:FOX_F_OFF + LANES]) + b_ref[...]
    log_f = jnp.minimum(f_logit, 0.0) - jnp.log(1.0 + jnp.exp(-jnp.abs(f_logit)))
    row = lax.broadcasted_iota(jnp.int32, (tm, tm), 0)
    col = lax.broadcasted_iota(jnp.int32, (tm, tm), 1)
    tri = jnp.where(row >= col, 1.0, 0.0).astype(jnp.bfloat16)
    hi, mid, lo = _split3(log_f)
    parts = jnp.concatenate([hi, mid, lo], axis=1).astype(jnp.bfloat16)
    cs = _dot(tri, parts)
    local = cs[:, :LANES] + cs[:, LANES:2 * LANES] + cs[:, 2 * LANES:]

    @pl.when(i == 0)
    def _():
        carry_ref[...] = jnp.zeros_like(carry_ref)

    f_cum = local + carry_ref[0:1, :]
    carry_ref[0:1, :] = f_cum[tm - 1:tm, :]
    f_log2 = f_cum * LOG2E
    f_cols = _split3(f_log2)
    f_rows = _split3(f_log2.T)

    k = _dot(xn, w_ref[:, K_OFF:K_OFF + SELF_WIDTH])
    brow = lax.broadcasted_iota(jnp.int32, (SEL_ROW, tm), 0)
    tail = jnp.zeros((HEAD_DIM - SEL_ROW, tm), jnp.float32)
    for p in range(N_PAIRS):
        kp = k[:, LANES * p:LANES * (p + 1)]
        for e in range(2):
            hd = 2 * p + e
            base = HEAD_DIM * (1 - e)
            bias = jnp.where((brow >= 3) & (brow < 6), 1.0, 0.0)
            ext_k = jnp.where((lane >= base) & (lane < base + 3), 1.0, 0.0)
            for c in range(3):
                bias = jnp.where(brow == c, f_rows[c][hd:hd + 1], bias)
                ext_k = jnp.where(lane == base + 3 + c, -f_cols[c][:, hd:hd + 1], ext_k)
            data = qvt[HEAD_DIM * hd:HEAD_DIM * (hd + 1)] * (SCALE * LOG2E)
            parts = [data, bias, tail] if e == 0 else [bias, tail, data]
            qt_ref[0, LANES * hd:LANES * (hd + 1), :] = (
                jnp.concatenate(parts, axis=0).astype(qt_ref.dtype))
            in_head = (lane >= HEAD_DIM * e) & (lane < HEAD_DIM * (e + 1))
            ka_ref[:, LANES * hd:LANES * (hd + 1)] = (
                jnp.where(in_head, kp, ext_k).astype(ka_ref.dtype))


def _moba_proj_kernel(h_ref, g_ref, w_ref, wqvt_ref, mk_ref, mv_ref,
                      qt_ref, ka_ref, vt_ref, hm_ref, km_ref, *, slopes):
    i = pl.program_id(1)
    tm = h_ref.shape[0]
    xn = _rms(h_ref[...], g_ref[...]).astype(jnp.bfloat16)
    lane = lax.broadcasted_iota(jnp.int32, (tm, LANES), 1)
    lane_row = lax.broadcasted_iota(jnp.int32, (1, LANES), 1)

    qvt = _nt_dot(wqvt_ref[...], xn)
    _write_vt(vt_ref, qvt[SELF_WIDTH:])
    qm = _dot(xn, w_ref[:, MOBA_QM_OFF:MOBA_QM_OFF + MEM_WIDTH])
    _memory_attention(qm, mk_ref, mv_ref, hm_ref, lane)

    @pl.when(i == 0)
    def _():
        km_ref[...] = jnp.zeros_like(km_ref)

    k = _dot(xn, w_ref[:, K_OFF:K_OFF + SELF_WIDTH])
    kmean = jnp.mean(k, axis=0, keepdims=True)

    block_start = (i * tm).astype(jnp.float32)
    offset = lax.broadcasted_iota(jnp.int32, (tm, LANES), 0).astype(jnp.float32)
    ext_k = []
    for e in range(2):
        base = HEAD_DIM * (1 - e)
        x = jnp.where((lane >= base) & (lane < base + 3), 1.0, 0.0)
        x = jnp.where((lane >= base + 4) & (lane < base + 7), block_start, x)
        x = jnp.where((lane >= base + 7) & (lane < base + 10), offset, x)
        ext_k.append(jnp.where(lane == base + SEL_ROW + i, 1.0, x))

    pos_row = (i * tm + lax.broadcasted_iota(jnp.int32, (1, tm), 1)).astype(jnp.float32)
    blk = lax.broadcasted_iota(jnp.int32, (MAX_BLOCKS, tm), 0)
    brow = lax.broadcasted_iota(jnp.int32, (SEL_ROW, tm), 0)
    tail = jnp.zeros((HEAD_DIM - SEL_ROW - MAX_BLOCKS, tm), jnp.float32)

    for p in range(N_PAIRS):
        qt_pair = qvt[LANES * p:LANES * (p + 1)]
        kp = k[:, LANES * p:LANES * (p + 1)]
        q_hi = qt_pair.astype(jnp.bfloat16)
        q_lo = (qt_pair - q_hi.astype(jnp.float32)).astype(jnp.bfloat16)
        km = km_ref[p]
        km_hi = km.astype(jnp.bfloat16)
        km_lo = (km - km_hi.astype(jnp.float32)).astype(jnp.bfloat16)
        r = _dot(jnp.concatenate([km_hi, km_lo], axis=0), q_hi)
        gates = r[:2 * MAX_BLOCKS] + r[2 * MAX_BLOCKS:] + _dot(km_hi, q_lo)
        for e in range(2):
            hd = 2 * p + e
            g = jnp.where(blk < i, gates[MAX_BLOCKS * e:MAX_BLOCKS * (e + 1)], BELOW_NEG)
            chosen = blk == i
            for _ in range(MOBA_TOPK):
                best = jnp.max(g, axis=0, keepdims=True)
                first = jnp.min(jnp.where(g == best, blk, 2 * MAX_BLOCKS),
                                axis=0, keepdims=True)
                hit = blk == first
                chosen = chosen | (hit & (best > 0.5 * BELOW_NEG))
                g = jnp.where(hit, BELOW_NEG, g)
            pen = jnp.where(chosen, 0.0, NEG)

            slope = slopes[hd] * LOG2E
            t_terms = _split3(-slope * pos_row)
            s_terms = _split3(jnp.full((1, tm), slope, jnp.float32))
            bias = jnp.where(brow == 3, 1.0, 0.0)
            for c in range(3):
                bias = jnp.where(brow == c, t_terms[c], bias)
                bias = jnp.where((brow == 4 + c) | (brow == 7 + c), s_terms[c], bias)
            ext_q = jnp.concatenate([bias, pen, tail], axis=0)
            data = qt_pair[HEAD_DIM * e:HEAD_DIM * (e + 1)] * (SCALE * LOG2E)
            parts = [data, ext_q] if e == 0 else [ext_q, data]
            qt_ref[0, LANES * hd:LANES * (hd + 1), :] = (
                jnp.concatenate(parts, axis=0).astype(qt_ref.dtype))

            in_head = (lane >= HEAD_DIM * e) & (lane < HEAD_DIM * (e + 1))
            ka_ref[:, LANES * hd:LANES * (hd + 1)] = (
                jnp.where(in_head, kp, ext_k[e]).astype(ka_ref.dtype))

            in_head_row = (lane_row >= HEAD_DIM * e) & (lane_row < HEAD_DIM * (e + 1))
            km_row = jnp.where(in_head_row, kmean[:, LANES * p:LANES * (p + 1)], 0.0)
            km_ref[p, pl.ds(MAX_BLOCKS * e + i, 1), :] = km_row


def _flash_kernel(qt_ref, ka_ref, vt_ref, kpad_ref, o_ref,
                  kbuf, m_ref, acc_ref, s0, s1, p0, p1, a0, a1):
    n_sub = qt_ref.shape[0]
    tq = qt_ref.shape[2]
    tk = vt_ref.shape[2]
    nq = vt_ref.shape[0]
    seq = ka_ref.shape[0]

    @pl.when(pl.program_id(2) == 0)
    def _():
        kbuf[0:seq, :] = ka_ref[...]
        kbuf[seq:seq + tk, :] = kpad_ref[...]

    for sub in range(n_sub):
        _attend_query_tile(n_sub * pl.program_id(2) + sub, nq, qt_ref.at[sub],
                           o_ref.at[tq * sub:tq * (sub + 1)], vt_ref,
                           kbuf, m_ref, acc_ref, s0, s1, p0, p1, a0, a1)


def _attend_query_tile(i, nq, qt_ref, o_ref, vt_ref, kbuf, m_ref, acc_ref,
                       s0, s1, p0, p1, a0, a1):
    tq = qt_ref.shape[1]
    tk = vt_ref.shape[2]

    def scores(tile, s_ref):
        rows = pl.ds(pl.multiple_of(tile * tk, tk), tk)
        for e in range(2):
            s_ref[e] = _dot(kbuf[rows, LANES * e:LANES * (e + 1)],
                            qt_ref[LANES * e:LANES * (e + 1), :])

    def softmax(s_ref, p_ref, a_ref, causal=False):
        for e in range(2):
            st = s_ref[e]
            if causal:
                kv = lax.broadcasted_iota(jnp.int32, (tk, tq), 0)
                qi = lax.broadcasted_iota(jnp.int32, (tk, tq), 1)
                st = jnp.where(kv <= qi, st, NEG)
            m_prev = m_ref[e]
            m_new = jnp.maximum(m_prev, jnp.max(st, axis=0, keepdims=True))
            a_ref[e] = jnp.exp2(m_prev - m_new)
            p_ref[e] = jnp.exp2(st - m_new).astype(p_ref.dtype)
            m_ref[e] = m_new

    def values(tile, p_ref, a_ref):
        for e in range(2):
            vt = vt_ref[tile, V_ROWS * e:V_ROWS * (e + 1), :]
            acc_ref[e] = a_ref[e] * acc_ref[e] + _dot(vt, p_ref[e])

    def key_tile(pos):
        return jnp.where(pos < i, pos, nq)

    def value_tile(pos):
        return jnp.where(pos < 0, i, jnp.minimum(pos, nq - 1))

    def pipeline_step(h):
        pos = 2 * h
        scores(key_tile(pos + 1), s1)
        values(value_tile(pos - 1), p1, a1)
        softmax(s0, p0, a0)
        scores(key_tile(pos + 2), s0)
        values(value_tile(pos), p0, a0)
        softmax(s1, p1, a1)

    m_ref[...] = jnp.full(m_ref.shape, NEG, jnp.float32)
    acc_ref[...] = jnp.zeros(acc_ref.shape, jnp.float32)
    scores(i, s1)
    scores(key_tile(0), s0)
    softmax(s1, p1, a1, causal=True)
    pipeline_step(0)

    rest = i // 2
    n_unrolled = rest // FLASH_UNROLL

    def unrolled(t, carry):
        for u in range(FLASH_UNROLL):
            pipeline_step(1 + FLASH_UNROLL * t + u)
        return carry

    def single(t, carry):
        pipeline_step(1 + FLASH_UNROLL * n_unrolled + t)
        return carry

    lax.fori_loop(0, n_unrolled, unrolled, 0)
    lax.fori_loop(0, rest - FLASH_UNROLL * n_unrolled, single, 0)

    outs = []
    for e in range(2):
        acc = acc_ref[e]
        outs.append(acc[:HEAD_DIM] / acc[HEAD_DIM:HEAD_DIM + 1])
    o_ref[...] = jnp.concatenate(outs, axis=0).T.astype(o_ref.dtype)


def _ffn_kernel(hs_ref, hm_ref, h_ref, woa_ref, wob_ref, g_ref, wgu_ref, wd_ref,
                gf_ref, o_ref, *, final):
    h1 = h_ref[...] + _dot(hs_ref[...], woa_ref[...]) + _dot(hm_ref[...], wob_ref[...])
    hn = _rms(h1, g_ref[...]).astype(jnp.bfloat16)
    ffn = None
    for c0, c1 in ((0, 1536), (1536, D_FF)):
        gate = _dot(hn, wgu_ref[:, c0:c1])
        up = _dot(hn, wgu_ref[:, D_FF + c0:D_FF + c1])
        act = (gate * jax.nn.sigmoid(gate) * up).astype(jnp.bfloat16)
        down = _dot(act, wd_ref[c0:c1, :])
        ffn = down if ffn is None else ffn + down
    y = h1 + ffn
    if final:
        y = _rms(y, gf_ref[...])
    o_ref[...] = y


def _mem_kv_kernel(mem_ref, g_ref, w_ref, o_ref):
    mn = _rms(mem_ref[...], g_ref[0]).astype(jnp.bfloat16)
    o_ref[0] = _dot(mn, w_ref[0]).astype(o_ref.dtype)


def _params(sem, flags=None):
    return pltpu.CompilerParams(dimension_semantics=sem, vmem_limit_bytes=VMEM_LIMIT,
                                flags=flags)


def _const_spec(shape):
    nd = len(shape)
    return pl.BlockSpec(shape, lambda *_: (0,) * nd)


def _mem_kv(mem2, norm_mem, w_mem_kv, batch):
    depth = w_mem_kv.shape[0]
    return pl.pallas_call(
        _mem_kv_kernel,
        grid=(depth, batch),
        in_specs=[pl.BlockSpec((N_MEM, D_MODEL), lambda d, b: (b, 0)),
                  pl.BlockSpec((1, 1, D_MODEL), lambda d, b: (d, 0, 0)),
                  pl.BlockSpec((1, D_MODEL, 2 * MEM_WIDTH), lambda d, b: (d, 0, 0))],
        out_specs=pl.BlockSpec((1, N_MEM, 2 * MEM_WIDTH), lambda d, b: (d, b, 0)),
        out_shape=jax.ShapeDtypeStruct((depth, batch * N_MEM, 2 * MEM_WIDTH), jnp.bfloat16),
        compiler_params=_params(("arbitrary", "arbitrary")),
        name="mem_kv",
    )(mem2, norm_mem.reshape(depth, 1, D_MODEL), w_mem_kv)


def _proj_call(kernel_fn, name, h, gain, w, extra, mkv, batch, seq, scratch):
    wt = jnp.concatenate([w[:, Q_OFF:Q_OFF + SELF_WIDTH],
                          w[:, V_OFF:V_OFF + SELF_WIDTH]], axis=1).T
    nt = seq // TOKEN_TILE
    tokens = batch * seq
    tok = lambda b, i: (b * nt + i, 0)
    tile = lambda b, i: (b * nt + i, 0, 0)
    bf16 = jnp.bfloat16
    in_specs = [pl.BlockSpec((TOKEN_TILE, D_MODEL), tok),
                _const_spec((1, D_MODEL)),
                _const_spec(w.shape),
                _const_spec(wt.shape)]
    args = [h, gain, w, wt]
    for a in extra:
        in_specs.append(_const_spec(a.shape))
        args.append(a)
    in_specs += [pl.BlockSpec((N_MEM, MEM_WIDTH), lambda b, i: (b, 0)),
                 pl.BlockSpec((N_MEM, MEM_WIDTH), lambda b, i: (b, 1))]
    args += [mkv, mkv]
    out_specs = [pl.BlockSpec((1, AUG_WIDTH, TOKEN_TILE), tile),
                 pl.BlockSpec((TOKEN_TILE, AUG_WIDTH), tok),
                 pl.BlockSpec((1, N_SELF_HEADS * V_ROWS, TOKEN_TILE), tile),
                 pl.BlockSpec((TOKEN_TILE, MEM_WIDTH), tok)]
    out_shape = [jax.ShapeDtypeStruct((tokens // TOKEN_TILE, AUG_WIDTH, TOKEN_TILE), bf16),
                 jax.ShapeDtypeStruct((tokens, AUG_WIDTH), bf16),
                 jax.ShapeDtypeStruct((tokens // TOKEN_TILE, N_SELF_HEADS * V_ROWS, TOKEN_TILE),
                                      bf16),
                 jax.ShapeDtypeStruct((tokens, MEM_WIDTH), bf16)]
    return pl.pallas_call(
        kernel_fn,
        grid=(batch, nt),
        in_specs=in_specs,
        out_specs=out_specs,
        out_shape=out_shape,
        scratch_shapes=scratch,
        compiler_params=_params(("arbitrary", "arbitrary")),
        name=name,
    )(*args)


def _flash(qt, ka, vt, batch, seq):
    nq = seq // TOKEN_TILE
    tokens = batch * seq
    tq = TOKEN_TILE
    f32, bf16 = jnp.float32, jnp.bfloat16
    lane = jnp.arange(2 * LANES)
    ones_lane = (lane == HEAD_DIM + 3) | (lane == LANES + 3)
    kpad = jnp.broadcast_to(jnp.where(ones_lane, NEG, 0.0), (tq, 2 * LANES)).astype(bf16)
    n_sub = FLASH_QUERY_TILES
    steps = nq // n_sub
    return pl.pallas_call(
        _flash_kernel,
        grid=(batch, N_PAIRS, steps),
        in_specs=[pl.BlockSpec((n_sub, 2 * LANES, tq), lambda b, p, i: (b * steps + i, p, 0)),
                  pl.BlockSpec((seq, 2 * LANES), lambda b, p, i: (b, p)),
                  pl.BlockSpec((nq, 2 * V_ROWS, tq), lambda b, p, i: (b, p, 0)),
                  _const_spec(kpad.shape)],
        out_specs=pl.BlockSpec((n_sub * tq, LANES), lambda b, p, i: (b * steps + i, p)),
        out_shape=jax.ShapeDtypeStruct((tokens, SELF_WIDTH), bf16),
        scratch_shapes=[pltpu.VMEM((seq + tq, 2 * LANES), bf16),
                        pltpu.VMEM((2, 1, tq), f32), pltpu.VMEM((2, V_ROWS, tq), f32),
                        pltpu.VMEM((2, tq, tq), f32), pltpu.VMEM((2, tq, tq), f32),
                        pltpu.VMEM((2, tq, tq), bf16), pltpu.VMEM((2, tq, tq), bf16),
                        pltpu.VMEM((2, 1, tq), f32), pltpu.VMEM((2, 1, tq), f32)],
        compiler_params=_params(("arbitrary", "arbitrary", "arbitrary")),
        name="flash_attention",
    )(qt, ka, vt, kpad)


def _ffn(hs, hm, h, woa, wob, g, wgu, wd, gf, final):
    tokens = h.shape[0]
    tok = lambda t: (t, 0)
    return pl.pallas_call(
        functools.partial(_ffn_kernel, final=final),
        grid=(tokens // FFN_TILE,),
        in_specs=[pl.BlockSpec((FFN_TILE, SELF_WIDTH), tok),
                  pl.BlockSpec((FFN_TILE, MEM_WIDTH), tok),
                  pl.BlockSpec((FFN_TILE, D_MODEL), tok),
                  _const_spec(woa.shape), _const_spec(wob.shape),
                  _const_spec((1, D_MODEL)),
                  _const_spec(wgu.shape), _const_spec(wd.shape),
                  _const_spec((1, D_MODEL))],
        out_specs=pl.BlockSpec((FFN_TILE, D_MODEL), tok),
        out_shape=jax.ShapeDtypeStruct((tokens, D_MODEL), jnp.float32),
        compiler_params=_params(("arbitrary",)),
        name="out_proj_ffn",
    )(hs, hm, h, woa, wob, g, wgu, wd, gf)


def kernel(x, mem, norm_mix, norm_mem, norm_ffn, norm_final, w_in_fox, b_fgate,
           w_in_moba, w_mem_kv, w_out, w_gate_up, w_down):
    batch, seq, _ = x.shape
    depth = w_out.shape[0]
    bf16 = jnp.bfloat16
    assert seq % (TOKEN_TILE * FLASH_QUERY_TILES) == 0 and seq // MOBA_BLOCK <= MAX_BLOCKS
    assert (batch * seq) % FFN_TILE == 0

    slopes = tuple(2.0 ** (-8.0 * (hd + 1) / N_SELF_HEADS) for hd in range(N_SELF_HEADS))
    h = x.reshape(batch * seq, D_MODEL)
    mkv_all = _mem_kv(mem.reshape(batch * N_MEM, D_MODEL), norm_mem,
                      w_mem_kv.astype(bf16), batch)
    gf = norm_final.reshape(1, D_MODEL)

    for i in range(depth):
        j = i // 2
        gain = norm_mix[i].reshape(1, D_MODEL)
        if i % 2 == 0:
            w = w_in_fox[j]
            f_cols = jnp.pad(w[:, FOX_F_OFF:FOX_F_OFF + N_SELF_HEADS],
                             ((0, 0), (0, LANES - N_SELF_HEADS)))
            w = jnp.concatenate([w[:, :FOX_F_OFF], f_cols,
                                 w[:, FOX_F_OFF + N_SELF_HEADS:]], axis=1).astype(bf16)
            bias = jnp.pad(b_fgate[j], (0, LANES - N_SELF_HEADS)).reshape(1, LANES)
            qt, ka, vt, hm = _proj_call(
                _fox_proj_kernel, "fox_proj", h, gain, w, [bias], mkv_all[i], batch, seq,
                [pltpu.VMEM((8, LANES), jnp.float32)])
        else:
            qt, ka, vt, hm = _proj_call(
                functools.partial(_moba_proj_kernel, slopes=slopes), "moba_proj",
                h, gain, w_in_moba[j].astype(bf16), [], mkv_all[i], batch, seq,
                [pltpu.VMEM((N_PAIRS, 2 * MAX_BLOCKS, LANES), jnp.float32)])
        hs = _flash(qt, ka, vt, batch, seq)
        wo = w_out[i].astype(bf16)
        h = _ffn(hs, hm, h, wo[:SELF_WIDTH], wo[SELF_WIDTH:],
                 norm_ffn[i].reshape(1, D_MODEL), w_gate_up[i].astype(bf16),
                 w_down[i].astype(bf16), gf, final=(i == depth - 1))
    return h.reshape(batch, seq, D_MODEL)
```

```python
import functools

import jax
import jax.numpy as jnp
from jax import lax
from jax.experimental import pallas as pl
from jax.experimental.pallas import tpu as pltpu

D_MODEL = 1024
N_SELF_HEADS = 12
N_MEM_HEADS = 4
HEAD_DIM = 64
SELF_WIDTH = N_SELF_HEADS * HEAD_DIM
MEM_WIDTH = N_MEM_HEADS * HEAD_DIM
N_MEM = 256
D_FF = 2816
MOBA_BLOCK = 256
MOBA_TOPK = 3
RMS_EPS = 1e-6
NEG = -1e30
BELOW_NEG = -3e38

LANES = 128
N_PAIRS = N_SELF_HEADS // 2
AUG_WIDTH = N_SELF_HEADS * LANES
TOKEN_TILE = MOBA_BLOCK
FFN_TILE = 256
FLASH_UNROLLS = (8, 4, 2, 1)
FLASH_QUERY_TILES = 2
MAX_BLOCKS = 32
SEL_ROW = 16
SCALE = HEAD_DIM ** -0.5
LOG2E = 1.4426950408889634
V_ROWS = 80
VMEM_LIMIT = 56 * 1024 * 1024

Q_OFF, K_OFF, V_OFF = 0, SELF_WIDTH, 2 * SELF_WIDTH
FOX_F_OFF = 3 * SELF_WIDTH
FOX_QM_OFF = FOX_F_OFF + LANES
FOX_PROJ_PAD = FOX_QM_OFF + MEM_WIDTH
MOBA_QM_OFF = 3 * SELF_WIDTH
MOBA_PROJ = MOBA_QM_OFF + MEM_WIDTH


def _nt_dot(a, b):
    return lax.dot_general(a, b, (((1,), (1,)), ((), ())),
                           preferred_element_type=jnp.float32)


def _dot(a, b):
    return jnp.dot(a, b, preferred_element_type=jnp.float32)


def _rms(xf, g):
    ms = jnp.mean(xf * xf, axis=-1, keepdims=True)
    return xf * lax.rsqrt(ms + RMS_EPS) * g


def _split3(x):
    hi = x.astype(jnp.bfloat16).astype(jnp.float32)
    r = x - hi
    mid = r.astype(jnp.bfloat16).astype(jnp.float32)
    return hi, mid, r - mid


def _memory_attention(qm, mk_ref, mv_ref, hm_ref, lane):
    for p in range(N_MEM_HEADS // 2):
        blk = qm[:, LANES * p:LANES * (p + 1)]
        mk = mk_ref[:, LANES * p:LANES * (p + 1)]
        mv = mv_ref[:, LANES * p:LANES * (p + 1)]
        outs = []
        for e in range(2):
            in_head = (lane >= HEAD_DIM * e) & (lane < HEAD_DIM * (e + 1))
            qh = jnp.where(in_head, blk, 0.0).astype(jnp.bfloat16)
            s = _nt_dot(qh, mk) * SCALE
            s = s - jnp.max(s, axis=-1, keepdims=True)
            pr = jnp.exp(s)
            den = jnp.sum(pr, axis=-1, keepdims=True)
            outs.append(_dot(pr.astype(jnp.bfloat16), mv) / den)
        out = jnp.where(lane < HEAD_DIM, outs[0], outs[1])
        hm_ref[:, LANES * p:LANES * (p + 1)] = out.astype(hm_ref.dtype)


def _write_vt(vt_ref, vt):
    tm = vt.shape[1]
    pad_rows = V_ROWS - HEAD_DIM
    ones_blk = jnp.where(lax.broadcasted_iota(jnp.int32, (pad_rows, tm), 0) == 0, 1.0, 0.0)
    for hd in range(N_SELF_HEADS):
        vt_ref[0, V_ROWS * hd:V_ROWS * hd + HEAD_DIM, :] = (
            vt[HEAD_DIM * hd:HEAD_DIM * (hd + 1)].astype(vt_ref.dtype))
        vt_ref[0, V_ROWS * hd + HEAD_DIM:V_ROWS * (hd + 1), :] = ones_blk.astype(vt_ref.dtype)


def _fox_proj_kernel(h_ref, g_ref, w_ref, wqvt_ref, b_ref, mk_ref, mv_ref,
                     qt_ref, ka_ref, vt_ref, hm_ref, carry_ref):
    i = pl.program_id(1)
    tm = h_ref.shape[0]
    xn = _rms(h_ref[...], g_ref[...]).astype(jnp.bfloat16)
    lane = lax.broadcasted_iota(jnp.int32, (tm, LANES), 1)

    f_logit = _dot(xn, w_ref[:, FOX_F_OFF:FOX_F_OFF + LANES]) + b_ref[...]
    log_f = jnp.minimum(f_logit, 0.0) - jnp.log(1.0 + jnp.exp(-jnp.abs(f_logit)))
    row = lax.broadcasted_iota(jnp.int32, (tm, tm), 0)
    col = lax.broadcasted_iota(jnp.int32, (tm, tm), 1)
    tri = jnp.where(row >= col, 1.0, 0.0).astype(jnp.bfloat16)
    hi, mid, lo = _split3(log_f)
    parts = jnp.concatenate([hi, mid, lo], axis=1).astype(jnp.bfloat16)
    cs = _dot(tri, parts)
    local = cs[:, :LANES] + cs[:, LANES:2 * LANES] + cs[:, 2 * LANES:]

    @pl.when(i == 0)
    def _():
        carry_ref[...] = jnp.zeros_like(carry_ref)

    f_cum = local + carry_ref[0:1, :]
    carry_ref[0:1, :] = f_cum[tm - 1:tm, :]
    f_log2 = f_cum * LOG2E
    f_cols = _split3(f_log2)
    f_rows = _split3(f_log2.T)

    qvt = _nt_dot(wqvt_ref[...], xn)
    _write_vt(vt_ref, qvt[SELF_WIDTH:])
    qm = _dot(xn, w_ref[:, FOX_QM_OFF:FOX_QM_OFF + MEM_WIDTH])
    _memory_attention(qm, mk_ref, mv_ref, hm_ref, lane)
    k = _dot(xn, w_ref[:, K_OFF:K_OFF + SELF_WIDTH])
    brow = lax.broadcasted_iota(jnp.int32, (SEL_ROW, tm), 0)
    tail = jnp.zeros((HEAD_DIM - SEL_ROW, tm), jnp.float32)
    for p in range(N_PAIRS):
        kp = k[:, LANES * p:LANES * (p + 1)]
        for e in range(2):
            hd = 2 * p + e
            base = HEAD_DIM * (1 - e)
            bias = jnp.where((brow >= 3) & (brow < 6), 1.0, 0.0)
            ext_k = jnp.where((lane >= base) & (lane < base + 3), 1.0, 0.0)
            for c in range(3):
                bias = jnp.where(brow == c, f_rows[c][hd:hd + 1], bias)
                ext_k = jnp.where(lane == base + 3 + c, -f_cols[c][:, hd:hd + 1], ext_k)
            data = qvt[HEAD_DIM * hd:HEAD_DIM * (hd + 1)] * (SCALE * LOG2E)
            parts = [data, bias, tail] if e == 0 else [bias, tail, data]
            qt_ref[0, LANES * hd:LANES * (hd + 1), :] = (
                jnp.concatenate(parts, axis=0).astype(qt_ref.dtype))
            in_head = (lane >= HEAD_DIM * e) & (lane < HEAD_DIM * (e + 1))
            ka_ref[:, LANES * hd:LANES * (hd + 1)] = (
                jnp.where(in_head, kp, ext_k).astype(ka_ref.dtype))


def _moba_proj_kernel(h_ref, g_ref, w_ref, wqvt_ref, mk_ref, mv_ref,
                      qt_ref, ka_ref, vt_ref, hm_ref, km_ref, *, slopes):
    i = pl.program_id(1)
    tm = h_ref.shape[0]
    xn = _rms(h_ref[...], g_ref[...]).astype(jnp.bfloat16)
    lane = lax.broadcasted_iota(jnp.int32, (tm, LANES), 1)
    lane_row = lax.broadcasted_iota(jnp.int32, (1, LANES), 1)

    qvt = _nt_dot(wqvt_ref[...], xn)
    _write_vt(vt_ref, qvt[SELF_WIDTH:])
    qm = _dot(xn, w_ref[:, MOBA_QM_OFF:MOBA_QM_OFF + MEM_WIDTH])
    _memory_attention(qm, mk_ref, mv_ref, hm_ref, lane)

    @pl.when(i == 0)
    def _():
        km_ref[...] = jnp.zeros_like(km_ref)

    k = _dot(xn, w_ref[:, K_OFF:K_OFF + SELF_WIDTH])
    kmean = jnp.mean(k, axis=0, keepdims=True)

    block_start = (i * tm).astype(jnp.float32)
    offset = lax.broadcasted_iota(jnp.int32, (tm, LANES), 0).astype(jnp.float32)
    ext_k = []
    for e in range(2):
        base = HEAD_DIM * (1 - e)
        x = jnp.where((lane >= base) & (lane < base + 3), 1.0, 0.0)
        x = jnp.where((lane >= base + 4) & (lane < base + 7), block_start, x)
        x = jnp.where((lane >= base + 7) & (lane < base + 10), offset, x)
        ext_k.append(jnp.where(lane == base + SEL_ROW + i, 1.0, x))

    pos_row = (i * tm + lax.broadcasted_iota(jnp.int32, (1, tm), 1)).astype(jnp.float32)
    blk = lax.broadcasted_iota(jnp.int32, (MAX_BLOCKS, tm), 0)
    brow = lax.broadcasted_iota(jnp.int32, (SEL_ROW, tm), 0)
    tail = jnp.zeros((HEAD_DIM - SEL_ROW - MAX_BLOCKS, tm), jnp.float32)

    for p in range(N_PAIRS):
        qt_pair = qvt[LANES * p:LANES * (p + 1)]
        kp = k[:, LANES * p:LANES * (p + 1)]
        q_hi = qt_pair.astype(jnp.bfloat16)
        q_lo = (qt_pair - q_hi.astype(jnp.float32)).astype(jnp.bfloat16)
        km = km_ref[p]
        km_hi = km.astype(jnp.bfloat16)
        km_lo = (km - km_hi.astype(jnp.float32)).astype(jnp.bfloat16)
        r = _dot(jnp.concatenate([km_hi, km_lo], axis=0), q_hi)
        gates = r[:2 * MAX_BLOCKS] + r[2 * MAX_BLOCKS:] + _dot(km_hi, q_lo)
        for e in range(2):
            hd = 2 * p + e
            g = jnp.where(blk < i, gates[MAX_BLOCKS * e:MAX_BLOCKS * (e + 1)], BELOW_NEG)
            chosen = blk == i
            for _ in range(MOBA_TOPK):
                best = jnp.max(g, axis=0, keepdims=True)
                first = jnp.min(jnp.where(g == best, blk, 2 * MAX_BLOCKS),
                                axis=0, keepdims=True)
                hit = blk == first
                chosen = chosen | (hit & (best > 0.5 * BELOW_NEG))
                g = jnp.where(hit, BELOW_NEG, g)
            pen = jnp.where(chosen, 0.0, NEG)

            slope = slopes[hd] * LOG2E
            t_terms = _split3(-slope * pos_row)
            s_terms = _split3(jnp.full((1, tm), slope, jnp.float32))
            bias = jnp.where(brow == 3, 1.0, 0.0)
            for c in range(3):
                bias = jnp.where(brow == c, t_terms[c], bias)
                bias = jnp.where((brow == 4 + c) | (brow == 7 + c), s_terms[c], bias)
            ext_q = jnp.concatenate([bias, pen, tail], axis=0)
            data = qt_pair[HEAD_DIM * e:HEAD_DIM * (e + 1)] * (SCALE * LOG2E)
            parts = [data, ext_q] if e == 0 else [ext_q, data]
            qt_ref[0, LANES * hd:LANES * (hd + 1), :] = (
                jnp.concatenate(parts, axis=0).astype(qt_ref.dtype))

            in_head = (lane >= HEAD_DIM * e) & (lane < HEAD_DIM * (e + 1))
            ka_ref[:, LANES * hd:LANES * (hd + 1)] = (
                jnp.where(in_head, kp, ext_k[e]).astype(ka_ref.dtype))

            in_head_row = (lane_row >= HEAD_DIM * e) & (lane_row < HEAD_DIM * (e + 1))
            km_row = jnp.where(in_head_row, kmean[:, LANES * p:LANES * (p + 1)], 0.0)
            km_ref[p, pl.ds(MAX_BLOCKS * e + i, 1), :] = km_row


def _flash_kernel(qt_ref, ka_ref, vt_ref, kpad_ref, o_ref,
                  kbuf, m_ref, acc_ref, s0, s1, p0, p1, a0, a1):
    n_sub = qt_ref.shape[0]
    tq = qt_ref.shape[2]
    tk = vt_ref.shape[2]
    nq = vt_ref.shape[0]
    seq = ka_ref.shape[0]

    @pl.when(pl.program_id(2) == 0)
    def _():
        kbuf[0:seq, :] = ka_ref[...]
        kbuf[seq:seq + tk, :] = kpad_ref[...]

    for sub in range(n_sub):
        _attend_query_tile(n_sub * pl.program_id(2) + sub, nq, qt_ref.at[sub],
                           o_ref.at[tq * sub:tq * (sub + 1)], vt_ref,
                           kbuf, m_ref, acc_ref, s0, s1, p0, p1, a0, a1)


def _attend_query_tile(i, nq, qt_ref, o_ref, vt_ref, kbuf, m_ref, acc_ref,
                       s0, s1, p0, p1, a0, a1):
    tq = qt_ref.shape[1]
    tk = vt_ref.shape[2]

    def scores(tile, s_ref):
        rows = pl.ds(pl.multiple_of(tile * tk, tk), tk)
        for e in range(2):
            s_ref[e] = _dot(kbuf[rows, LANES * e:LANES * (e + 1)],
                            qt_ref[LANES * e:LANES * (e + 1), :])

    def softmax(s_ref, p_ref, a_ref, causal=False):
        for e in range(2):
            st = s_ref[e]
            if causal:
                kv = lax.broadcasted_iota(jnp.int32, (tk, tq), 0)
                qi = lax.broadcasted_iota(jnp.int32, (tk, tq), 1)
                st = jnp.where(kv <= qi, st, NEG)
            m_prev = m_ref[e]
            m_new = jnp.maximum(m_prev, jnp.max(st, axis=0, keepdims=True))
            a_ref[e] = jnp.exp2(m_prev - m_new)
            p_ref[e] = jnp.exp2(st - m_new).astype(p_ref.dtype)
            m_ref[e] = m_new

    def values(tile, p_ref, a_ref):
        for e in range(2):
            vt = vt_ref[tile, V_ROWS * e:V_ROWS * (e + 1), :]
            acc_ref[e] = a_ref[e] * acc_ref[e] + _dot(vt, p_ref[e])

    def key_tile(pos):
        return jnp.where(pos < i, pos, nq)

    def value_tile(pos):
        return jnp.where(pos < 0, i, jnp.minimum(pos, nq - 1))

    def pipeline_step(h):
        pos = 2 * h
        scores(key_tile(pos + 1), s1)
        values(value_tile(pos - 1), p1, a1)
        softmax(s0, p0, a0)
        scores(key_tile(pos + 2), s0)
        values(value_tile(pos), p0, a0)
        softmax(s1, p1, a1)

    m_ref[...] = jnp.full(m_ref.shape, NEG, jnp.float32)
    acc_ref[...] = jnp.zeros(acc_ref.shape, jnp.float32)
    scores(i, s1)
    scores(key_tile(0), s0)
    softmax(s1, p1, a1, causal=True)
    pipeline_step(0)

    done = 1
    for unroll in FLASH_UNROLLS:
        trips = (i // 2 + 1 - done) // unroll

        def body(t, carry, unroll=unroll, done=done):
            for u in range(unroll):
                pipeline_step(done + unroll * t + u)
            return carry

        lax.fori_loop(0, trips, body, 0)
        done = done + unroll * trips

    outs = []
    for e in range(2):
        acc = acc_ref[e]
        outs.append(acc[:HEAD_DIM] / acc[HEAD_DIM:HEAD_DIM + 1])
    o_ref[...] = jnp.concatenate(outs, axis=0).T.astype(o_ref.dtype)


def _ffn_kernel(hs_ref, hm_ref, h_ref, woa_ref, wob_ref, g_ref, wgu_ref, wd_ref,
                gf_ref, o_ref, *, final):
    h1 = h_ref[...] + _dot(hs_ref[...], woa_ref[...]) + _dot(hm_ref[...], wob_ref[...])
    hn = _rms(h1, g_ref[...]).astype(jnp.bfloat16)
    ffn = None
    for c0, c1 in ((0, 1536), (1536, D_FF)):
        gate = _dot(hn, wgu_ref[:, c0:c1])
        up = _dot(hn, wgu_ref[:, D_FF + c0:D_FF + c1])
        act = (gate * jax.nn.sigmoid(gate) * up).astype(jnp.bfloat16)
        down = _dot(act, wd_ref[c0:c1, :])
        ffn = down if ffn is None else ffn + down
    y = h1 + ffn
    if final:
        y = _rms(y, gf_ref[...])
    o_ref[...] = y


def _mem_kv_kernel(mem_ref, g_ref, w_ref, o_ref):
    mn = _rms(mem_ref[...], g_ref[0]).astype(jnp.bfloat16)
    o_ref[0] = _dot(mn, w_ref[0]).astype(o_ref.dtype)


def _params(sem, flags=None):
    return pltpu.CompilerParams(dimension_semantics=sem, vmem_limit_bytes=VMEM_LIMIT,
                                flags=flags)


def _const_spec(shape):
    nd = len(shape)
    return pl.BlockSpec(shape, lambda *_: (0,) * nd)


def _mem_kv(mem2, norm_mem, w_mem_kv, batch):
    depth = w_mem_kv.shape[0]
    return pl.pallas_call(
        _mem_kv_kernel,
        grid=(depth, batch),
        in_specs=[pl.BlockSpec((N_MEM, D_MODEL), lambda d, b: (b, 0)),
                  pl.BlockSpec((1, 1, D_MODEL), lambda d, b: (d, 0, 0)),
                  pl.BlockSpec((1, D_MODEL, 2 * MEM_WIDTH), lambda d, b: (d, 0, 0))],
        out_specs=pl.BlockSpec((1, N_MEM, 2 * MEM_WIDTH), lambda d, b: (d, b, 0)),
        out_shape=jax.ShapeDtypeStruct((depth, batch * N_MEM, 2 * MEM_WIDTH), jnp.bfloat16),
        compiler_params=_params(("arbitrary", "arbitrary")),
        name="mem_kv",
    )(mem2, norm_mem.reshape(depth, 1, D_MODEL), w_mem_kv)


def _proj_call(kernel_fn, name, h, gain, w, extra, mkv, batch, seq, scratch):
    wt = jnp.concatenate([w[:, Q_OFF:Q_OFF + SELF_WIDTH],
                          w[:, V_OFF:V_OFF + SELF_WIDTH]], axis=1).T
    nt = seq // TOKEN_TILE
    tokens = batch * seq
    tok = lambda b, i: (b * nt + i, 0)
    tile = lambda b, i: (b * nt + i, 0, 0)
    bf16 = jnp.bfloat16
    in_specs = [pl.BlockSpec((TOKEN_TILE, D_MODEL), tok),
                _const_spec((1, D_MODEL)),
                _const_spec(w.shape),
                _const_spec(wt.shape)]
    args = [h, gain, w, wt]
    for a in extra:
        in_specs.append(_const_spec(a.shape))
        args.append(a)
    in_specs += [pl.BlockSpec((N_MEM, MEM_WIDTH), lambda b, i: (b, 0)),
                 pl.BlockSpec((N_MEM, MEM_WIDTH), lambda b, i: (b, 1))]
    args += [mkv, mkv]
    out_specs = [pl.BlockSpec((1, AUG_WIDTH, TOKEN_TILE), tile),
                 pl.BlockSpec((TOKEN_TILE, AUG_WIDTH), tok),
                 pl.BlockSpec((1, N_SELF_HEADS * V_ROWS, TOKEN_TILE), tile),
                 pl.BlockSpec((TOKEN_TILE, MEM_WIDTH), tok)]
    out_shape = [jax.ShapeDtypeStruct((tokens // TOKEN_TILE, AUG_WIDTH, TOKEN_TILE), bf16),
                 jax.ShapeDtypeStruct((tokens, AUG_WIDTH), bf16),
                 jax.ShapeDtypeStruct((tokens // TOKEN_TILE, N_SELF_HEADS * V_ROWS, TOKEN_TILE),
                                      bf16),
                 jax.ShapeDtypeStruct((tokens, MEM_WIDTH), bf16)]
    return pl.pallas_call(
        kernel_fn,
        grid=(batch, nt),
        in_specs=in_specs,
        out_specs=out_specs,
        out_shape=out_shape,
        scratch_shapes=scratch,
        compiler_params=_params(("arbitrary", "arbitrary")),
        name=name,
    )(*args)


def _flash(qt, ka, vt, batch, seq):
    nq = seq // TOKEN_TILE
    tokens = batch * seq
    tq = TOKEN_TILE
    f32, bf16 = jnp.float32, jnp.bfloat16
    lane = jnp.arange(2 * LANES)
    ones_lane = (lane == HEAD_DIM + 3) | (lane == LANES + 3)
    kpad = jnp.broadcast_to(jnp.where(ones_lane, NEG, 0.0), (tq, 2 * LANES)).astype(bf16)
    n_sub = FLASH_QUERY_TILES
    steps = nq // n_sub
    return pl.pallas_call(
        _flash_kernel,
        grid=(batch, N_PAIRS, steps),
        in_specs=[pl.BlockSpec((n_sub, 2 * LANES, tq), lambda b, p, i: (b * steps + i, p, 0)),
                  pl.BlockSpec((seq, 2 * LANES), lambda b, p, i: (b, p)),
                  pl.BlockSpec((nq, 2 * V_ROWS, tq), lambda b, p, i: (b, p, 0)),
                  _const_spec(kpad.shape)],
        out_specs=pl.BlockSpec((n_sub * tq, LANES), lambda b, p, i: (b * steps + i, p)),
        out_shape=jax.ShapeDtypeStruct((tokens, SELF_WIDTH), bf16),
        scratch_shapes=[pltpu.VMEM((seq + tq, 2 * LANES), bf16),
                        pltpu.VMEM((2, 1, tq), f32), pltpu.VMEM((2, V_ROWS, tq), f32),
                        pltpu.VMEM((2, tq, tq), f32), pltpu.VMEM((2, tq, tq), f32),
                        pltpu.VMEM((2, tq, tq), bf16), pltpu.VMEM((2, tq, tq), bf16),
                        pltpu.VMEM((2, 1, tq), f32), pltpu.VMEM((2, 1, tq), f32)],
        compiler_params=_params(("arbitrary", "arbitrary", "arbitrary")),
        name="flash_attention",
    )(qt, ka, vt, kpad)


def _ffn(hs, hm, h, woa, wob, g, wgu, wd, gf, final):
    tokens = h.shape[0]
    tok = lambda t: (t, 0)
    return pl.pallas_call(
        functools.partial(_ffn_kernel, final=final),
        grid=(tokens // FFN_TILE,),
        in_specs=[pl.BlockSpec((FFN_TILE, SELF_WIDTH), tok),
                  pl.BlockSpec((FFN_TILE, MEM_WIDTH), tok),
                  pl.BlockSpec((FFN_TILE, D_MODEL), tok),
                  _const_spec(woa.shape), _const_spec(wob.shape),
                  _const_spec((1, D_MODEL)),
                  _const_spec(wgu.shape), _const_spec(wd.shape),
                  _const_spec((1, D_MODEL))],
        out_specs=pl.BlockSpec((FFN_TILE, D_MODEL), tok),
        out_shape=jax.ShapeDtypeStruct((tokens, D_MODEL), jnp.float32),
        compiler_params=_params(("arbitrary",)),
        name="out_proj_ffn",
    )(hs, hm, h, woa, wob, g, wgu, wd, gf)


def kernel(x, mem, norm_mix, norm_mem, norm_ffn, norm_final, w_in_fox, b_fgate,
           w_in_moba, w_mem_kv, w_out, w_gate_up, w_down):
    batch, seq, _ = x.shape
    depth = w_out.shape[0]
    bf16 = jnp.bfloat16
    assert seq % (TOKEN_TILE * FLASH_QUERY_TILES) == 0 and seq // MOBA_BLOCK <= MAX_BLOCKS
    assert (batch * seq) % FFN_TILE == 0

    slopes = tuple(2.0 ** (-8.0 * (hd + 1) / N_SELF_HEADS) for hd in range(N_SELF_HEADS))
    h = x.reshape(batch * seq, D_MODEL)
    mkv_all = _mem_kv(mem.reshape(batch * N_MEM, D_MODEL), norm_mem,
                      w_mem_kv.astype(bf16), batch)
    gf = norm_final.reshape(1, D_MODEL)

    for i in range(depth):
        j = i // 2
        gain = norm_mix[i].reshape(1, D_MODEL)
        if i % 2 == 0:
            w = w_in_fox[j]
            f_cols = jnp.pad(w[:, FOX_F_OFF:FOX_F_OFF + N_SELF_HEADS],
                             ((0, 0), (0, LANES - N_SELF_HEADS)))
            w = jnp.concatenate([w[:, :FOX_F_OFF], f_cols,
                                 w[:, FOX_F_OFF + N_SELF_HEADS:]], axis=1).astype(bf16)
            bias = jnp.pad(b_fgate[j], (0, LANES - N_SELF_HEADS)).reshape(1, LANES)
            qt, ka, vt, hm = _proj_call(
                _fox_proj_kernel, "fox_proj", h, gain, w, [bias], mkv_all[i], batch, seq,
                [pltpu.VMEM((8, LANES), jnp.float32)])
        else:
            qt, ka, vt, hm = _proj_call(
                functools.partial(_moba_proj_kernel, slopes=slopes), "moba_proj",
                h, gain, w_in_moba[j].astype(bf16), [], mkv_all[i], batch, seq,
                [pltpu.VMEM((N_PAIRS, 2 * MAX_BLOCKS, LANES), jnp.float32)])
        hs = _flash(qt, ka, vt, batch, seq)
        wo = w_out[i].astype(bf16)
        h = _ffn(hs, hm, h, wo[:SELF_WIDTH], wo[SELF_WIDTH:],
                 norm_ffn[i].reshape(1, D_MODEL), w_gate_up[i].astype(bf16),
                 w_down[i].astype(bf16), gf, final=(i == depth - 1))
    return h.reshape(batch, seq, D_MODEL)
```

```python
import functools

import jax
import jax.numpy as jnp
from jax import lax
from jax.experimental import pallas as pl
from jax.experimental.pallas import tpu as pltpu

D_MODEL = 1024
N_SELF_HEADS = 12
N_MEM_HEADS = 4
HEAD_DIM = 64
SELF_WIDTH = N_SELF_HEADS * HEAD_DIM
MEM_WIDTH = N_MEM_HEADS * HEAD_DIM
N_MEM = 256
D_FF = 2816
MOBA_BLOCK = 256
MOBA_TOPK = 3
RMS_EPS = 1e-6
NEG = -1e30
BELOW_NEG = -3e38

LANES = 128
N_PAIRS = N_SELF_HEADS // 2
AUG_WIDTH = N_SELF_HEADS * LANES
TOKEN_TILE = MOBA_BLOCK
FFN_TILE = 512
FLASH_UNROLLS = (8, 4, 2, 1)
FLASH_QUERY_TILES = 4
MAX_BLOCKS = 32
SEL_ROW = 16
SCALE = HEAD_DIM ** -0.5
LOG2E = 1.4426950408889634
V_ROWS = 80
VMEM_LIMIT = 56 * 1024 * 1024

Q_OFF, K_OFF, V_OFF = 0, SELF_WIDTH, 2 * SELF_WIDTH
FOX_F_OFF = 3 * SELF_WIDTH
FOX_QM_OFF = FOX_F_OFF + LANES
FOX_PROJ_PAD = FOX_QM_OFF + MEM_WIDTH
MOBA_QM_OFF = 3 * SELF_WIDTH
MOBA_PROJ = MOBA_QM_OFF + MEM_WIDTH


def _nt_dot(a, b):
    return lax.dot_general(a, b, (((1,), (1,)), ((), ())),
                           preferred_element_type=jnp.float32)


def _dot(a, b):
    return jnp.dot(a, b, preferred_element_type=jnp.float32)


def _rms(xf, g):
    ms = jnp.mean(xf * xf, axis=-1, keepdims=True)
    return xf * lax.rsqrt(ms + RMS_EPS) * g


def _split3(x):
    hi = x.astype(jnp.bfloat16).astype(jnp.float32)
    r = x - hi
    mid = r.astype(jnp.bfloat16).astype(jnp.float32)
    return hi, mid, r - mid


def _memory_attention(qm, mk_ref, mv_ref, hm_ref, lane):
    for p in range(N_MEM_HEADS // 2):
        blk = qm[:, LANES * p:LANES * (p + 1)]
        mk = mk_ref[:, LANES * p:LANES * (p + 1)]
        mv = mv_ref[:, LANES * p:LANES * (p + 1)]
        outs = []
        for e in range(2):
            in_head = (lane >= HEAD_DIM * e) & (lane < HEAD_DIM * (e + 1))
            qh = jnp.where(in_head, blk, 0.0).astype(jnp.bfloat16)
            s = _nt_dot(qh, mk) * SCALE
            s = s - jnp.max(s, axis=-1, keepdims=True)
            pr = jnp.exp(s)
            den = jnp.sum(pr, axis=-1, keepdims=True)
            outs.append(_dot(pr.astype(jnp.bfloat16), mv) / den)
        out = jnp.where(lane < HEAD_DIM, outs[0], outs[1])
        hm_ref[:, LANES * p:LANES * (p + 1)] = out.astype(hm_ref.dtype)


def _write_vt(vt_ref, vt):
    tm = vt.shape[1]
    pad_rows = V_ROWS - HEAD_DIM
    ones_blk = jnp.where(lax.broadcasted_iota(jnp.int32, (pad_rows, tm), 0) == 0, 1.0, 0.0)
    for hd in range(N_SELF_HEADS):
        vt_ref[0, V_ROWS * hd:V_ROWS * hd + HEAD_DIM, :] = (
            vt[HEAD_DIM * hd:HEAD_DIM * (hd + 1)].astype(vt_ref.dtype))
        vt_ref[0, V_ROWS * hd + HEAD_DIM:V_ROWS * (hd + 1), :] = ones_blk.astype(vt_ref.dtype)


def _fox_proj_kernel(h_ref, g_ref, w_ref, wqvt_ref, b_ref, mk_ref, mv_ref,
                     qt_ref, ka_ref, vt_ref, hm_ref, carry_ref):
    i = pl.program_id(1)
    tm = h_ref.shape[0]
    xn = _rms(h_ref[...], g_ref[...]).astype(jnp.bfloat16)
    lane = lax.broadcasted_iota(jnp.int32, (tm, LANES), 1)

    f_logit = _dot(xn, w_ref[:, FOX_F_OFF:FOX_F_OFF + LANES]) + b_ref[...]
    log_f = jnp.minimum(f_logit, 0.0) - jnp.log(1.0 + jnp.exp(-jnp.abs(f_logit)))
    row = lax.broadcasted_iota(jnp.int32, (tm, tm), 0)
    col = lax.broadcasted_iota(jnp.int32, (tm, tm), 1)
    tri = jnp.where(row >= col, 1.0, 0.0).astype(jnp.bfloat16)
    hi, mid, lo = _split3(log_f)
    parts = jnp.concatenate([hi, mid, lo], axis=1).astype(jnp.bfloat16)
    cs = _dot(tri, parts)
    local = cs[:, :LANES] + cs[:, LANES:2 * LANES] + cs[:, 2 * LANES:]

    @pl.when(i == 0)
    def _():
        carry_ref[...] = jnp.zeros_like(carry_ref)

    f_cum = local + carry_ref[0:1, :]
    carry_ref[0:1, :] = f_cum[tm - 1:tm, :]
    f_log2 = f_cum * LOG2E
    f_cols = _split3(f_log2)
    f_rows = _split3(f_log2.T)

    qvt = _nt_dot(wqvt_ref[...], xn)
    _write_vt(vt_ref, qvt[SELF_WIDTH:])
    qm = _dot(xn, w_ref[:, FOX_QM_OFF:FOX_QM_OFF + MEM_WIDTH])
    _memory_attention(qm, mk_ref, mv_ref, hm_ref, lane)
    k = _dot(xn, w_ref[:, K_OFF:K_OFF + SELF_WIDTH])
    brow = lax.broadcasted_iota(jnp.int32, (SEL_ROW, tm), 0)
    tail = jnp.zeros((HEAD_DIM - SEL_ROW, tm), jnp.float32)
    for p in range(N_PAIRS):
        kp = k[:, LANES * p:LANES * (p + 1)]
        for e in range(2):
            hd = 2 * p + e
            base = HEAD_DIM * (1 - e)
            bias = jnp.where((brow >= 3) & (brow < 6), 1.0, 0.0)
            ext_k = jnp.where((lane >= base) & (lane < base + 3), 1.0, 0.0)
            for c in range(3):
                bias = jnp.where(brow == c, f_rows[c][hd:hd + 1], bias)
                ext_k = jnp.where(lane == base + 3 + c, -f_cols[c][:, hd:hd + 1], ext_k)
            data = qvt[HEAD_DIM * hd:HEAD_DIM * (hd + 1)] * (SCALE * LOG2E)
            parts = [data, bias, tail] if e == 0 else [bias, tail, data]
            qt_ref[0, LANES * hd:LANES * (hd + 1), :] = (
                jnp.concatenate(parts, axis=0).astype(qt_ref.dtype))
            in_head = (lane >= HEAD_DIM * e) & (lane < HEAD_DIM * (e + 1))
            ka_ref[:, LANES * hd:LANES * (hd + 1)] = (
                jnp.where(in_head, kp, ext_k).astype(ka_ref.dtype))


def _moba_proj_kernel(h_ref, g_ref, w_ref, wqvt_ref, mk_ref, mv_ref,
                      qt_ref, ka_ref, vt_ref, hm_ref, km_ref, *, slopes):
    i = pl.program_id(1)
    tm = h_ref.shape[0]
    xn = _rms(h_ref[...], g_ref[...]).astype(jnp.bfloat16)
    lane = lax.broadcasted_iota(jnp.int32, (tm, LANES), 1)
    lane_row = lax.broadcasted_iota(jnp.int32, (1, LANES), 1)

    qvt = _nt_dot(wqvt_ref[...], xn)
    _write_vt(vt_ref, qvt[SELF_WIDTH:])
    qm = _dot(xn, w_ref[:, MOBA_QM_OFF:MOBA_QM_OFF + MEM_WIDTH])
    _memory_attention(qm, mk_ref, mv_ref, hm_ref, lane)

    @pl.when(i == 0)
    def _():
        km_ref[...] = jnp.zeros_like(km_ref)

    k = _dot(xn, w_ref[:, K_OFF:K_OFF + SELF_WIDTH])
    kmean = jnp.mean(k, axis=0, keepdims=True)

    block_start = (i * tm).astype(jnp.float32)
    offset = lax.broadcasted_iota(jnp.int32, (tm, LANES), 0).astype(jnp.float32)
    ext_k = []
    for e in range(2):
        base = HEAD_DIM * (1 - e)
        x = jnp.where((lane >= base) & (lane < base + 3), 1.0, 0.0)
        x = jnp.where((lane >= base + 4) & (lane < base + 7), block_start, x)
        x = jnp.where((lane >= base + 7) & (lane < base + 10), offset, x)
        ext_k.append(jnp.where(lane == base + SEL_ROW + i, 1.0, x))

    pos_row = (i * tm + lax.broadcasted_iota(jnp.int32, (1, tm), 1)).astype(jnp.float32)
    blk = lax.broadcasted_iota(jnp.int32, (MAX_BLOCKS, tm), 0)
    brow = lax.broadcasted_iota(jnp.int32, (SEL_ROW, tm), 0)
    tail = jnp.zeros((HEAD_DIM - SEL_ROW - MAX_BLOCKS, tm), jnp.float32)

    for p in range(N_PAIRS):
        qt_pair = qvt[LANES * p:LANES * (p + 1)]
        kp = k[:, LANES * p:LANES * (p + 1)]
        q_hi = qt_pair.astype(jnp.bfloat16)
        q_lo = (qt_pair - q_hi.astype(jnp.float32)).astype(jnp.bfloat16)
        km = km_ref[p]
        km_hi = km.astype(jnp.bfloat16)
        km_lo = (km - km_hi.astype(jnp.float32)).astype(jnp.bfloat16)
        r = _dot(jnp.concatenate([km_hi, km_lo], axis=0), q_hi)
        gates = r[:2 * MAX_BLOCKS] + r[2 * MAX_BLOCKS:] + _dot(km_hi, q_lo)
        for e in range(2):
            hd = 2 * p + e
            g = jnp.where(blk < i, gates[MAX_BLOCKS * e:MAX_BLOCKS * (e + 1)], BELOW_NEG)
            chosen = blk == i
            for _ in range(MOBA_TOPK):
                best = jnp.max(g, axis=0, keepdims=True)
                first = jnp.min(jnp.where(g == best, blk, 2 * MAX_BLOCKS),
                                axis=0, keepdims=True)
                hit = blk == first
                chosen = chosen | (hit & (best > 0.5 * BELOW_NEG))
                g = jnp.where(hit, BELOW_NEG, g)
            pen = jnp.where(chosen, 0.0, NEG)

            slope = slopes[hd] * LOG2E
            t_terms = _split3(-slope * pos_row)
            s_terms = _split3(jnp.full((1, tm), slope, jnp.float32))
            bias = jnp.where(brow == 3, 1.0, 0.0)
            for c in range(3):
                bias = jnp.where(brow == c, t_terms[c], bias)
                bias = jnp.where((brow == 4 + c) | (brow == 7 + c), s_terms[c], bias)
            ext_q = jnp.concatenate([bias, pen, tail], axis=0)
            data = qt_pair[HEAD_DIM * e:HEAD_DIM * (e + 1)] * (SCALE * LOG2E)
            parts = [data, ext_q] if e == 0 else [ext_q, data]
            qt_ref[0, LANES * hd:LANES * (hd + 1), :] = (
                jnp.concatenate(parts, axis=0).astype(qt_ref.dtype))

            in_head = (lane >= HEAD_DIM * e) & (lane < HEAD_DIM * (e + 1))
            ka_ref[:, LANES * hd:LANES * (hd + 1)] = (
                jnp.where(in_head, kp, ext_k[e]).astype(ka_ref.dtype))

            in_head_row = (lane_row >= HEAD_DIM * e) & (lane_row < HEAD_DIM * (e + 1))
            km_row = jnp.where(in_head_row, kmean[:, LANES * p:LANES * (p + 1)], 0.0)
            km_ref[p, pl.ds(MAX_BLOCKS * e + i, 1), :] = km_row


def _flash_kernel(qt_ref, ka_ref, vt_ref, kpad_ref, o_ref,
                  kbuf, m_ref, acc_ref, s0, s1, p0, p1, a0, a1):
    n_sub = qt_ref.shape[0]
    tq = qt_ref.shape[2]
    tk = vt_ref.shape[2]
    nq = vt_ref.shape[0]
    seq = ka_ref.shape[0]

    @pl.when(pl.program_id(2) == 0)
    def _():
        kbuf[0:seq, :] = ka_ref[...]
        kbuf[seq:seq + tk, :] = kpad_ref[...]

    for sub in range(n_sub):
        _attend_query_tile(n_sub * pl.program_id(2) + sub, nq, qt_ref.at[sub],
                           o_ref.at[tq * sub:tq * (sub + 1)], vt_ref,
                           kbuf, m_ref, acc_ref, s0, s1, p0, p1, a0, a1)


def _attend_query_tile(i, nq, qt_ref, o_ref, vt_ref, kbuf, m_ref, acc_ref,
                       s0, s1, p0, p1, a0, a1):
    tq = qt_ref.shape[1]
    tk = vt_ref.shape[2]

    def scores(tile, s_ref):
        rows = pl.ds(pl.multiple_of(tile * tk, tk), tk)
        for e in range(2):
            s_ref[e] = _dot(kbuf[rows, LANES * e:LANES * (e + 1)],
                            qt_ref[LANES * e:LANES * (e + 1), :])

    def softmax(s_ref, p_ref, a_ref, causal=False):
        for e in range(2):
            st = s_ref[e]
            if causal:
                kv = lax.broadcasted_iota(jnp.int32, (tk, tq), 0)
                qi = lax.broadcasted_iota(jnp.int32, (tk, tq), 1)
                st = jnp.where(kv <= qi, st, NEG)
            m_prev = m_ref[e]
            m_new = jnp.maximum(m_prev, jnp.max(st, axis=0, keepdims=True))
            a_ref[e] = jnp.exp2(m_prev - m_new)
            p_ref[e] = jnp.exp2(st - m_new).astype(p_ref.dtype)
            m_ref[e] = m_new

    def values(tile, p_ref, a_ref):
        for e in range(2):
            vt = vt_ref[tile, V_ROWS * e:V_ROWS * (e + 1), :]
            acc_ref[e] = a_ref[e] * acc_ref[e] + _dot(vt, p_ref[e])

    def key_tile(pos):
        return jnp.where(pos < i, pos, nq)

    def value_tile(pos):
        return jnp.where(pos < 0, i, jnp.minimum(pos, nq - 1))

    def pipeline_step(h):
        pos = 2 * h
        scores(key_tile(pos + 1), s1)
        values(value_tile(pos - 1), p1, a1)
        softmax(s0, p0, a0)
        scores(key_tile(pos + 2), s0)
        values(value_tile(pos), p0, a0)
        softmax(s1, p1, a1)

    m_ref[...] = jnp.full(m_ref.shape, NEG, jnp.float32)
    acc_ref[...] = jnp.zeros(acc_ref.shape, jnp.float32)
    scores(i, s1)
    scores(key_tile(0), s0)
    softmax(s1, p1, a1, causal=True)
    pipeline_step(0)

    done = 1
    for unroll in FLASH_UNROLLS:
        trips = (i // 2 + 1 - done) // unroll

        def body(t, carry, unroll=unroll, done=done):
            for u in range(unroll):
                pipeline_step(done + unroll * t + u)
            return carry

        lax.fori_loop(0, trips, body, 0)
        done = done + unroll * trips

    outs = []
    for e in range(2):
        acc = acc_ref[e]
        outs.append(acc[:HEAD_DIM] / acc[HEAD_DIM:HEAD_DIM + 1])
    o_ref[...] = jnp.concatenate(outs, axis=0).T.astype(o_ref.dtype)


def _ffn_kernel(hs_ref, hm_ref, h_ref, woa_ref, wob_ref, g_ref, wgu_ref, wd_ref,
                gf_ref, o_ref, *, final):
    h1 = h_ref[...] + _dot(hs_ref[...], woa_ref[...]) + _dot(hm_ref[...], wob_ref[...])
    hn = _rms(h1, g_ref[...]).astype(jnp.bfloat16)
    ffn = None
    for c0, c1 in ((0, 1536), (1536, D_FF)):
        gate = _dot(hn, wgu_ref[:, c0:c1])
        up = _dot(hn, wgu_ref[:, D_FF + c0:D_FF + c1])
        act = (gate * jax.nn.sigmoid(gate) * up).astype(jnp.bfloat16)
        down = _dot(act, wd_ref[c0:c1, :])
        ffn = down if ffn is None else ffn + down
    y = h1 + ffn
    if final:
        y = _rms(y, gf_ref[...])
    o_ref[...] = y


def _mem_kv_kernel(mem_ref, g_ref, w_ref, o_ref):
    mn = _rms(mem_ref[...], g_ref[0]).astype(jnp.bfloat16)
    o_ref[0] = _dot(mn, w_ref[0]).astype(o_ref.dtype)


def _params(sem, flags=None):
    return pltpu.CompilerParams(dimension_semantics=sem, vmem_limit_bytes=VMEM_LIMIT,
                                flags=flags)


def _const_spec(shape):
    nd = len(shape)
    return pl.BlockSpec(shape, lambda *_: (0,) * nd)


def _mem_kv(mem2, norm_mem, w_mem_kv, batch):
    depth = w_mem_kv.shape[0]
    return pl.pallas_call(
        _mem_kv_kernel,
        grid=(depth, batch),
        in_specs=[pl.BlockSpec((N_MEM, D_MODEL), lambda d, b: (b, 0)),
                  pl.BlockSpec((1, 1, D_MODEL), lambda d, b: (d, 0, 0)),
                  pl.BlockSpec((1, D_MODEL, 2 * MEM_WIDTH), lambda d, b: (d, 0, 0))],
        out_specs=pl.BlockSpec((1, N_MEM, 2 * MEM_WIDTH), lambda d, b: (d, b, 0)),
        out_shape=jax.ShapeDtypeStruct((depth, batch * N_MEM, 2 * MEM_WIDTH), jnp.bfloat16),
        compiler_params=_params(("arbitrary", "arbitrary")),
        name="mem_kv",
    )(mem2, norm_mem.reshape(depth, 1, D_MODEL), w_mem_kv)


def _proj_call(kernel_fn, name, h, gain, w, extra, mkv, batch, seq, scratch):
    wt = jnp.concatenate([w[:, Q_OFF:Q_OFF + SELF_WIDTH],
                          w[:, V_OFF:V_OFF + SELF_WIDTH]], axis=1).T
    nt = seq // TOKEN_TILE
    tokens = batch * seq
    tok = lambda b, i: (b * nt + i, 0)
    tile = lambda b, i: (b * nt + i, 0, 0)
    bf16 = jnp.bfloat16
    in_specs = [pl.BlockSpec((TOKEN_TILE, D_MODEL), tok),
                _const_spec((1, D_MODEL)),
                _const_spec(w.shape),
                _const_spec(wt.shape)]
    args = [h, gain, w, wt]
    for a in extra:
        in_specs.append(_const_spec(a.shape))
        args.append(a)
    in_specs += [pl.BlockSpec((N_MEM, MEM_WIDTH), lambda b, i: (b, 0)),
                 pl.BlockSpec((N_MEM, MEM_WIDTH), lambda b, i: (b, 1))]
    args += [mkv, mkv]
    out_specs = [pl.BlockSpec((1, AUG_WIDTH, TOKEN_TILE), tile),
                 pl.BlockSpec((TOKEN_TILE, AUG_WIDTH), tok),
                 pl.BlockSpec((1, N_SELF_HEADS * V_ROWS, TOKEN_TILE), tile),
                 pl.BlockSpec((TOKEN_TILE, MEM_WIDTH), tok)]
    out_shape = [jax.ShapeDtypeStruct((tokens // TOKEN_TILE, AUG_WIDTH, TOKEN_TILE), bf16),
                 jax.ShapeDtypeStruct((tokens, AUG_WIDTH), bf16),
                 jax.ShapeDtypeStruct((tokens // TOKEN_TILE, N_SELF_HEADS * V_ROWS, TOKEN_TILE),
                                      bf16),
                 jax.ShapeDtypeStruct((tokens, MEM_WIDTH), bf16)]
    return pl.pallas_call(
        kernel_fn,
        grid=(batch, nt),
        in_specs=in_specs,
        out_specs=out_specs,
        out_shape=out_shape,
        scratch_shapes=scratch,
        compiler_params=_params(("arbitrary", "arbitrary")),
        name=name,
    )(*args)


def _flash(qt, ka, vt, batch, seq):
    nq = seq // TOKEN_TILE
    tokens = batch * seq
    tq = TOKEN_TILE
    f32, bf16 = jnp.float32, jnp.bfloat16
    lane = jnp.arange(2 * LANES)
    ones_lane = (lane == HEAD_DIM + 3) | (lane == LANES + 3)
    kpad = jnp.broadcast_to(jnp.where(ones_lane, NEG, 0.0), (tq, 2 * LANES)).astype(bf16)
    n_sub = FLASH_QUERY_TILES
    steps = nq // n_sub
    return pl.pallas_call(
        _flash_kernel,
        grid=(batch, N_PAIRS, steps),
        in_specs=[pl.BlockSpec((n_sub, 2 * LANES, tq), lambda b, p, i: (b * steps + i, p, 0)),
                  pl.BlockSpec((seq, 2 * LANES), lambda b, p, i: (b, p)),
                  pl.BlockSpec((nq, 2 * V_ROWS, tq), lambda b, p, i: (b, p, 0)),
                  _const_spec(kpad.shape)],
        out_specs=pl.BlockSpec((n_sub * tq, LANES), lambda b, p, i: (b * steps + i, p)),
        out_shape=jax.ShapeDtypeStruct((tokens, SELF_WIDTH), bf16),
        scratch_shapes=[pltpu.VMEM((seq + tq, 2 * LANES), bf16),
                        pltpu.VMEM((2, 1, tq), f32), pltpu.VMEM((2, V_ROWS, tq), f32),
                        pltpu.VMEM((2, tq, tq), f32), pltpu.VMEM((2, tq, tq), f32),
                        pltpu.VMEM((2, tq, tq), bf16), pltpu.VMEM((2, tq, tq), bf16),
                        pltpu.VMEM((2, 1, tq), f32), pltpu.VMEM((2, 1, tq), f32)],
        compiler_params=_params(("arbitrary", "arbitrary", "arbitrary")),
        name="flash_attention",
    )(qt, ka, vt, kpad)


def _ffn(hs, hm, h, woa, wob, g, wgu, wd, gf, final):
    tokens = h.shape[0]
    tok = lambda t: (t, 0)
    return pl.pallas_call(
        functools.partial(_ffn_kernel, final=final),
        grid=(tokens // FFN_TILE,),
        in_specs=[pl.BlockSpec((FFN_TILE, SELF_WIDTH), tok),
                  pl.BlockSpec((FFN_TILE, MEM_WIDTH), tok),
                  pl.BlockSpec((FFN_TILE, D_MODEL), tok),
                  _const_spec(woa.shape), _const_spec(wob.shape),
                  _const_spec((1, D_MODEL)),
                  _const_spec(wgu.shape), _const_spec(wd.shape),
                  _const_spec((1, D_MODEL))],
        out_specs=pl.BlockSpec((FFN_TILE, D_MODEL), tok),
        out_shape=jax.ShapeDtypeStruct((tokens, D_MODEL), jnp.float32),
        compiler_params=_params(("arbitrary",)),
        name="out_proj_ffn",
    )(hs, hm, h, woa, wob, g, wgu, wd, gf)


def kernel(x, mem, norm_mix, norm_mem, norm_ffn, norm_final, w_in_fox, b_fgate,
           w_in_moba, w_mem_kv, w_out, w_gate_up, w_down):
    batch, seq, _ = x.shape
    depth = w_out.shape[0]
    bf16 = jnp.bfloat16
    assert seq % (TOKEN_TILE * FLASH_QUERY_TILES) == 0 and seq // MOBA_BLOCK <= MAX_BLOCKS
    assert (batch * seq) % FFN_TILE == 0

    slopes = tuple(2.0 ** (-8.0 * (hd + 1) / N_SELF_HEADS) for hd in range(N_SELF_HEADS))
    h = x.reshape(batch * seq, D_MODEL)
    mkv_all = _mem_kv(mem.reshape(batch * N_MEM, D_MODEL), norm_mem,
                      w_mem_kv.astype(bf16), batch)
    gf = norm_final.reshape(1, D_MODEL)

    for i in range(depth):
        j = i // 2
        gain = norm_mix[i].reshape(1, D_MODEL)
        if i % 2 == 0:
            w = w_in_fox[j]
            f_cols = jnp.pad(w[:, FOX_F_OFF:FOX_F_OFF + N_SELF_HEADS],
                             ((0, 0), (0, LANES - N_SELF_HEADS)))
            w = jnp.concatenate([w[:, :FOX_F_OFF], f_cols,
                                 w[:, FOX_F_OFF + N_SELF_HEADS:]], axis=1).astype(bf16)
            bias = jnp.pad(b_fgate[j], (0, LANES - N_SELF_HEADS)).reshape(1, LANES)
            qt, ka, vt, hm = _proj_call(
                _fox_proj_kernel, "fox_proj", h, gain, w, [bias], mkv_all[i], batch, seq,
                [pltpu.VMEM((8, LANES), jnp.float32)])
        else:
            qt, ka, vt, hm = _proj_call(
                functools.partial(_moba_proj_kernel, slopes=slopes), "moba_proj",
                h, gain, w_in_moba[j].astype(bf16), [], mkv_all[i], batch, seq,
                [pltpu.VMEM((N_PAIRS, 2 * MAX_BLOCKS, LANES), jnp.float32)])
        hs = _flash(qt, ka, vt, batch, seq)
        wo = w_out[i].astype(bf16)
        h = _ffn(hs, hm, h, wo[:SELF_WIDTH], wo[SELF_WIDTH:],
                 norm_ffn[i].reshape(1, D_MODEL), w_gate_up[i].astype(bf16),
                 w_down[i].astype(bf16), gf, final=(i == depth - 1))
    return h.reshape(batch, seq, D_MODEL)
```

```python
import functools

import jax
import jax.numpy as jnp
from jax import lax
from jax.experimental import pallas as pl
from jax.experimental.pallas import tpu as pltpu

D_MODEL = 1024
N_SELF_HEADS = 12
N_MEM_HEADS = 4
HEAD_DIM = 64
SELF_WIDTH = N_SELF_HEADS * HEAD_DIM
MEM_WIDTH = N_MEM_HEADS * HEAD_DIM
N_MEM = 256
D_FF = 2816
MOBA_BLOCK = 256
MOBA_TOPK = 3
RMS_EPS = 1e-6
NEG = -1e30
BELOW_NEG = -3e38

LANES = 128
N_PAIRS = N_SELF_HEADS // 2
AUG_WIDTH = N_SELF_HEADS * LANES
TOKEN_TILE = MOBA_BLOCK
FFN_TILE = 512
FLASH_UNROLLS = (8, 4, 2, 1)
FLASH_QUERY_TILES = 4
MAX_BLOCKS = 32
SEL_ROW = 16
SCALE = HEAD_DIM ** -0.5
LOG2E = 1.4426950408889634
V_ROWS = 80
VMEM_LIMIT = 56 * 1024 * 1024

Q_OFF, K_OFF, V_OFF = 0, SELF_WIDTH, 2 * SELF_WIDTH
FOX_F_OFF = 3 * SELF_WIDTH
FOX_QM_OFF = FOX_F_OFF + LANES
MOBA_QM_OFF = 3 * SELF_WIDTH


def _nt_dot(a, b):
    return lax.dot_general(a, b, (((1,), (1,)), ((), ())),
                           preferred_element_type=jnp.float32)


def _dot(a, b):
    return jnp.dot(a, b, preferred_element_type=jnp.float32)


def _rms(xf, g):
    ms = jnp.mean(xf * xf, axis=-1, keepdims=True)
    return xf * lax.rsqrt(ms + RMS_EPS) * g


def _split3(x):
    hi = x.astype(jnp.bfloat16).astype(jnp.float32)
    r = x - hi
    mid = r.astype(jnp.bfloat16).astype(jnp.float32)
    return hi, mid, r - mid


def _memory_attention(qm, mk_ref, mv_ref, hm_ref, lane):
    for p in range(N_MEM_HEADS // 2):
        blk = qm[:, LANES * p:LANES * (p + 1)]
        mk = mk_ref[:, LANES * p:LANES * (p + 1)]
        mv = mv_ref[:, LANES * p:LANES * (p + 1)]
        outs = []
        for e in range(2):
            in_head = (lane >= HEAD_DIM * e) & (lane < HEAD_DIM * (e + 1))
            qh = jnp.where(in_head, blk, 0.0).astype(jnp.bfloat16)
            s = _nt_dot(qh, mk) * SCALE
            s = s - jnp.max(s, axis=-1, keepdims=True)
            pr = jnp.exp(s)
            den = jnp.sum(pr, axis=-1, keepdims=True)
            outs.append(_dot(pr.astype(jnp.bfloat16), mv) / den)
        out = jnp.where(lane < HEAD_DIM, outs[0], outs[1])
        hm_ref[:, LANES * p:LANES * (p + 1)] = out.astype(hm_ref.dtype)


def _write_vt(vt_ref, vt):
    tm = vt.shape[1]
    pad_rows = V_ROWS - HEAD_DIM
    ones_blk = jnp.where(lax.broadcasted_iota(jnp.int32, (pad_rows, tm), 0) == 0, 1.0, 0.0)
    for hd in range(N_SELF_HEADS):
        vt_ref[0, V_ROWS * hd:V_ROWS * hd + HEAD_DIM, :] = (
            vt[HEAD_DIM * hd:HEAD_DIM * (hd + 1)].astype(vt_ref.dtype))
        vt_ref[0, V_ROWS * hd + HEAD_DIM:V_ROWS * (hd + 1), :] = ones_blk.astype(vt_ref.dtype)


def _fox_proj_kernel(h_ref, g_ref, w_ref, wqvt_ref, b_ref, mk_ref, mv_ref,
                     qt_ref, ka_ref, vt_ref, hm_ref, carry_ref):
    i = pl.program_id(1)
    tm = h_ref.shape[0]
    xn = _rms(h_ref[...], g_ref[...]).astype(jnp.bfloat16)
    lane = lax.broadcasted_iota(jnp.int32, (tm, LANES), 1)

    f_logit = _dot(xn, w_ref[:, FOX_F_OFF:FOX_F_OFF + LANES]) + b_ref[...]
    log_f = jnp.minimum(f_logit, 0.0) - jnp.log(1.0 + jnp.exp(-jnp.abs(f_logit)))
    row = lax.broadcasted_iota(jnp.int32, (tm, tm), 0)
    col = lax.broadcasted_iota(jnp.int32, (tm, tm), 1)
    tri = jnp.where(row >= col, 1.0, 0.0).astype(jnp.bfloat16)
    hi, mid, lo = _split3(log_f)
    cs = _dot(tri, jnp.concatenate([hi, mid, lo], axis=1).astype(jnp.bfloat16))
    local = cs[:, :LANES] + cs[:, LANES:2 * LANES] + cs[:, 2 * LANES:]

    @pl.when(i == 0)
    def _():
        carry_ref[...] = jnp.zeros_like(carry_ref)

    f_cum = local + carry_ref[0:1, :]
    carry_ref[0:1, :] = f_cum[tm - 1:tm, :]
    f_log2 = f_cum * LOG2E
    f_cols = _split3(f_log2)
    f_rows = _split3(f_log2.T)

    qvt = _nt_dot(wqvt_ref[...], xn)
    _write_vt(vt_ref, qvt[SELF_WIDTH:])
    qm = _dot(xn, w_ref[:, FOX_QM_OFF:FOX_QM_OFF + MEM_WIDTH])
    _memory_attention(qm, mk_ref, mv_ref, hm_ref, lane)
    k = _dot(xn, w_ref[:, K_OFF:K_OFF + SELF_WIDTH])
    brow = lax.broadcasted_iota(jnp.int32, (SEL_ROW, tm), 0)
    tail = jnp.zeros((HEAD_DIM - SEL_ROW, tm), jnp.float32)
    for p in range(N_PAIRS):
        kp = k[:, LANES * p:LANES * (p + 1)]
        for e in range(2):
            hd = 2 * p + e
            base = HEAD_DIM * (1 - e)
            bias = jnp.where((brow >= 3) & (brow < 6), 1.0, 0.0)
            ext_k = jnp.where((lane >= base) & (lane < base + 3), 1.0, 0.0)
            for c in range(3):
                bias = jnp.where(brow == c, f_rows[c][hd:hd + 1], bias)
                ext_k = jnp.where(lane == base + 3 + c, -f_cols[c][:, hd:hd + 1], ext_k)
            data = qvt[HEAD_DIM * hd:HEAD_DIM * (hd + 1)] * (SCALE * LOG2E)
            parts = [data, bias, tail] if e == 0 else [bias, tail, data]
            qt_ref[0, LANES * hd:LANES * (hd + 1), :] = (
                jnp.concatenate(parts, axis=0).astype(qt_ref.dtype))
            in_head = (lane >= HEAD_DIM * e) & (lane < HEAD_DIM * (e + 1))
            ka_ref[:, LANES * hd:LANES * (hd + 1)] = (
                jnp.where(in_head, kp, ext_k).astype(ka_ref.dtype))


def _moba_proj_kernel(h_ref, g_ref, w_ref, wqvt_ref, mk_ref, mv_ref,
                      qt_ref, ka_ref, vt_ref, hm_ref, km_ref, *, slopes):
    i = pl.program_id(1)
    tm = h_ref.shape[0]
    xn = _rms(h_ref[...], g_ref[...]).astype(jnp.bfloat16)
    lane = lax.broadcasted_iota(jnp.int32, (tm, LANES), 1)
    lane_row = lax.broadcasted_iota(jnp.int32, (1, LANES), 1)

    qvt = _nt_dot(wqvt_ref[...], xn)
    _write_vt(vt_ref, qvt[SELF_WIDTH:])
    qm = _dot(xn, w_ref[:, MOBA_QM_OFF:MOBA_QM_OFF + MEM_WIDTH])
    _memory_attention(qm, mk_ref, mv_ref, hm_ref, lane)

    @pl.when(i == 0)
    def _():
        km_ref[...] = jnp.zeros_like(km_ref)

    k = _dot(xn, w_ref[:, K_OFF:K_OFF + SELF_WIDTH])
    kmean = jnp.mean(k, axis=0, keepdims=True)

    block_start = (i * tm).astype(jnp.float32)
    offset = lax.broadcasted_iota(jnp.int32, (tm, LANES), 0).astype(jnp.float32)
    ext_k = []
    for e in range(2):
        base = HEAD_DIM * (1 - e)
        x = jnp.where((lane >= base) & (lane < base + 3), 1.0, 0.0)
        x = jnp.where((lane >= base + 4) & (lane < base + 7), block_start, x)
        x = jnp.where((lane >= base + 7) & (lane < base + 10), offset, x)
        ext_k.append(jnp.where(lane == base + SEL_ROW + i, 1.0, x))

    pos_row = (i * tm + lax.broadcasted_iota(jnp.int32, (1, tm), 1)).astype(jnp.float32)
    blk = lax.broadcasted_iota(jnp.int32, (MAX_BLOCKS, tm), 0)
    brow = lax.broadcasted_iota(jnp.int32, (SEL_ROW, tm), 0)
    tail = jnp.zeros((HEAD_DIM - SEL_ROW - MAX_BLOCKS, tm), jnp.float32)

    for p in range(N_PAIRS):
        qt_pair = qvt[LANES * p:LANES * (p + 1)]
        kp = k[:, LANES * p:LANES * (p + 1)]
        q_hi = qt_pair.astype(jnp.bfloat16)
        q_lo = (qt_pair - q_hi.astype(jnp.float32)).astype(jnp.bfloat16)
        km = km_ref[p]
        km_hi = km.astype(jnp.bfloat16)
        km_lo = (km - km_hi.astype(jnp.float32)).astype(jnp.bfloat16)
        r = _dot(jnp.concatenate([km_hi, km_lo], axis=0), q_hi)
        gates = r[:2 * MAX_BLOCKS] + r[2 * MAX_BLOCKS:] + _dot(km_hi, q_lo)
        for e in range(2):
            hd = 2 * p + e
            g = jnp.where(blk < i, gates[MAX_BLOCKS * e:MAX_BLOCKS * (e + 1)], BELOW_NEG)
            chosen = blk == i
            for _ in range(MOBA_TOPK):
                best = jnp.max(g, axis=0, keepdims=True)
                first = jnp.min(jnp.where(g == best, blk, 2 * MAX_BLOCKS),
                                axis=0, keepdims=True)
                hit = blk == first
                chosen = chosen | (hit & (best > 0.5 * BELOW_NEG))
                g = jnp.where(hit, BELOW_NEG, g)
            pen = jnp.where(chosen, 0.0, NEG)

            slope = slopes[hd] * LOG2E
            t_terms = _split3(-slope * pos_row)
            s_terms = _split3(jnp.full((1, tm), slope, jnp.float32))
            bias = jnp.where(brow == 3, 1.0, 0.0)
            for c in range(3):
                bias = jnp.where(brow == c, t_terms[c], bias)
                bias = jnp.where((brow == 4 + c) | (brow == 7 + c), s_terms[c], bias)
            ext_q = jnp.concatenate([bias, pen, tail], axis=0)
            data = qt_pair[HEAD_DIM * e:HEAD_DIM * (e + 1)] * (SCALE * LOG2E)
            parts = [data, ext_q] if e == 0 else [ext_q, data]
            qt_ref[0, LANES * hd:LANES * (hd + 1), :] = (
                jnp.concatenate(parts, axis=0).astype(qt_ref.dtype))

            in_head = (lane >= HEAD_DIM * e) & (lane < HEAD_DIM * (e + 1))
            ka_ref[:, LANES * hd:LANES * (hd + 1)] = (
                jnp.where(in_head, kp, ext_k[e]).astype(ka_ref.dtype))

            in_head_row = (lane_row >= HEAD_DIM * e) & (lane_row < HEAD_DIM * (e + 1))
            km_row = jnp.where(in_head_row, kmean[:, LANES * p:LANES * (p + 1)], 0.0)
            km_ref[p, pl.ds(MAX_BLOCKS * e + i, 1), :] = km_row


def _flash_kernel(qt_ref, ka_ref, vt_ref, kpad_ref, o_ref,
                  kbuf, m_ref, acc_ref, s0, s1, p0, p1, a0, a1):
    n_sub = qt_ref.shape[0]
    tq = qt_ref.shape[2]
    tk = vt_ref.shape[2]
    nq = vt_ref.shape[0]
    seq = ka_ref.shape[0]

    @pl.when(pl.program_id(2) == 0)
    def _():
        kbuf[0:seq, :] = ka_ref[...]
        kbuf[seq:seq + tk, :] = kpad_ref[...]

    for sub in range(n_sub):
        _attend_query_tile(n_sub * pl.program_id(2) + sub, nq, qt_ref.at[sub],
                           o_ref.at[tq * sub:tq * (sub + 1)], vt_ref,
                           kbuf, m_ref, acc_ref, s0, s1, p0, p1, a0, a1)


def _attend_query_tile(i, nq, qt_ref, o_ref, vt_ref, kbuf, m_ref, acc_ref,
                       s0, s1, p0, p1, a0, a1):
    tq = qt_ref.shape[1]
    tk = vt_ref.shape[2]

    def scores(tile, s_ref):
        rows = pl.ds(pl.multiple_of(tile * tk, tk), tk)
        for e in range(2):
            s_ref[e] = _dot(kbuf[rows, LANES * e:LANES * (e + 1)],
                            qt_ref[LANES * e:LANES * (e + 1), :])

    def softmax(s_ref, p_ref, a_ref, causal=False):
        for e in range(2):
            st = s_ref[e]
            if causal:
                kv = lax.broadcasted_iota(jnp.int32, (tk, tq), 0)
                qi = lax.broadcasted_iota(jnp.int32, (tk, tq), 1)
                st = jnp.where(kv <= qi, st, NEG)
            m_prev = m_ref[e]
            m_new = jnp.maximum(m_prev, jnp.max(st, axis=0, keepdims=True))
            a_ref[e] = jnp.exp2(m_prev - m_new)
            p_ref[e] = jnp.exp2(st - m_new).astype(p_ref.dtype)
            m_ref[e] = m_new

    def values(tile, p_ref, a_ref):
        for e in range(2):
            vt = vt_ref[tile, V_ROWS * e:V_ROWS * (e + 1), :]
            acc_ref[e] = a_ref[e] * acc_ref[e] + _dot(vt, p_ref[e])

    def key_tile(pos):
        return jnp.where(pos < i, pos, nq)

    def value_tile(pos):
        return jnp.where(pos < 0, i, jnp.minimum(pos, nq - 1))

    def pipeline_step(h):
        pos = 2 * h
        scores(key_tile(pos + 1), s1)
        values(value_tile(pos - 1), p1, a1)
        softmax(s0, p0, a0)
        scores(key_tile(pos + 2), s0)
        values(value_tile(pos), p0, a0)
        softmax(s1, p1, a1)

    m_ref[...] = jnp.full(m_ref.shape, NEG, jnp.float32)
    acc_ref[...] = jnp.zeros(acc_ref.shape, jnp.float32)
    scores(i, s1)
    scores(key_tile(0), s0)
    softmax(s1, p1, a1, causal=True)
    pipeline_step(0)

    done = 1
    for unroll in FLASH_UNROLLS:
        trips = (i // 2 + 1 - done) // unroll

        def body(t, carry, unroll=unroll, done=done):
            for u in range(unroll):
                pipeline_step(done + unroll * t + u)
            return carry

        lax.fori_loop(0, trips, body, 0)
        done = done + unroll * trips

    outs = []
    for e in range(2):
        acc = acc_ref[e]
        outs.append(acc[:HEAD_DIM] / acc[HEAD_DIM:HEAD_DIM + 1])
    o_ref[...] = jnp.concatenate(outs, axis=0).T.astype(o_ref.dtype)


def _ffn_kernel(hs_ref, hm_ref, h_ref, woa_ref, wob_ref, g_ref, wgu_ref, wd_ref,
                gf_ref, o_ref, *, final):
    h1 = h_ref[...] + _dot(hs_ref[...], woa_ref[...]) + _dot(hm_ref[...], wob_ref[...])
    hn = _rms(h1, g_ref[...]).astype(jnp.bfloat16)
    ffn = None
    for c0, c1 in ((0, 1536), (1536, D_FF)):
        gate = _dot(hn, wgu_ref[:, c0:c1])
        up = _dot(hn, wgu_ref[:, D_FF + c0:D_FF + c1])
        act = (gate * jax.nn.sigmoid(gate) * up).astype(jnp.bfloat16)
        down = _dot(act, wd_ref[c0:c1, :])
        ffn = down if ffn is None else ffn + down
    y = h1 + ffn
    if final:
        y = _rms(y, gf_ref[...])
    o_ref[...] = y


def _mem_kv_kernel(mem_ref, g_ref, w_ref, o_ref):
    mn = _rms(mem_ref[...], g_ref[0]).astype(jnp.bfloat16)
    o_ref[0] = _dot(mn, w_ref[0]).astype(o_ref.dtype)


def _params(sem):
    return pltpu.CompilerParams(dimension_semantics=sem, vmem_limit_bytes=VMEM_LIMIT)


def _const_spec(shape):
    nd = len(shape)
    return pl.BlockSpec(shape, lambda *_: (0,) * nd)


def _mem_kv(mem2, norm_mem, w_mem_kv, batch):
    depth = w_mem_kv.shape[0]
    return pl.pallas_call(
        _mem_kv_kernel,
        grid=(depth, batch),
        in_specs=[pl.BlockSpec((N_MEM, D_MODEL), lambda d, b: (b, 0)),
                  pl.BlockSpec((1, 1, D_MODEL), lambda d, b: (d, 0, 0)),
                  pl.BlockSpec((1, D_MODEL, 2 * MEM_WIDTH), lambda d, b: (d, 0, 0))],
        out_specs=pl.BlockSpec((1, N_MEM, 2 * MEM_WIDTH), lambda d, b: (d, b, 0)),
        out_shape=jax.ShapeDtypeStruct((depth, batch * N_MEM, 2 * MEM_WIDTH), jnp.bfloat16),
        compiler_params=_params(("arbitrary", "arbitrary")),
        name="mem_kv",
    )(mem2, norm_mem.reshape(depth, 1, D_MODEL), w_mem_kv)


def _proj_call(kernel_fn, name, h, gain, w, extra, mkv, batch, seq, scratch):
    wt = jnp.concatenate([w[:, Q_OFF:Q_OFF + SELF_WIDTH],
                          w[:, V_OFF:V_OFF + SELF_WIDTH]], axis=1).T
    nt = seq // TOKEN_TILE
    tokens = batch * seq
    tok = lambda b, i: (b * nt + i, 0)
    tile = lambda b, i: (b * nt + i, 0, 0)
    bf16 = jnp.bfloat16
    in_specs = [pl.BlockSpec((TOKEN_TILE, D_MODEL), tok),
                _const_spec((1, D_MODEL)),
                _const_spec(w.shape),
                _const_spec(wt.shape)]
    args = [h, gain, w, wt]
    for a in extra:
        in_specs.append(_const_spec(a.shape))
        args.append(a)
    in_specs += [pl.BlockSpec((N_MEM, MEM_WIDTH), lambda b, i: (b, 0)),
                 pl.BlockSpec((N_MEM, MEM_WIDTH), lambda b, i: (b, 1))]
    args += [mkv, mkv]
    out_specs = [pl.BlockSpec((1, AUG_WIDTH, TOKEN_TILE), tile),
                 pl.BlockSpec((TOKEN_TILE, AUG_WIDTH), tok),
                 pl.BlockSpec((1, N_SELF_HEADS * V_ROWS, TOKEN_TILE), tile),
                 pl.BlockSpec((TOKEN_TILE, MEM_WIDTH), tok)]
    out_shape = [jax.ShapeDtypeStruct((tokens // TOKEN_TILE, AUG_WIDTH, TOKEN_TILE), bf16),
                 jax.ShapeDtypeStruct((tokens, AUG_WIDTH), bf16),
                 jax.ShapeDtypeStruct((tokens // TOKEN_TILE, N_SELF_HEADS * V_ROWS, TOKEN_TILE),
                                      bf16),
                 jax.ShapeDtypeStruct((tokens, MEM_WIDTH), bf16)]
    return pl.pallas_call(
        kernel_fn,
        grid=(batch, nt),
        in_specs=in_specs,
        out_specs=out_specs,
        out_shape=out_shape,
        scratch_shapes=scratch,
        compiler_params=_params(("arbitrary", "arbitrary")),
        name=name,
    )(*args)


def _flash(qt, ka, vt, batch, seq):
    nq = seq // TOKEN_TILE
    tokens = batch * seq
    tq = TOKEN_TILE
    f32, bf16 = jnp.float32, jnp.bfloat16
    lane = jnp.arange(2 * LANES)
    ones_lane = (lane == HEAD_DIM + 3) | (lane == LANES + 3)
    kpad = jnp.broadcast_to(jnp.where(ones_lane, NEG, 0.0), (tq, 2 * LANES)).astype(bf16)
    n_sub = FLASH_QUERY_TILES
    steps = nq // n_sub
    return pl.pallas_call(
        _flash_kernel,
        grid=(batch, N_PAIRS, steps),
        in_specs=[pl.BlockSpec((n_sub, 2 * LANES, tq), lambda b, p, i: (b * steps + i, p, 0)),
                  pl.BlockSpec((seq, 2 * LANES), lambda b, p, i: (b, p)),
                  pl.BlockSpec((nq, 2 * V_ROWS, tq), lambda b, p, i: (b, p, 0)),
                  _const_spec(kpad.shape)],
        out_specs=pl.BlockSpec((n_sub * tq, LANES), lambda b, p, i: (b * steps + i, p)),
        out_shape=jax.ShapeDtypeStruct((tokens, SELF_WIDTH), bf16),
        scratch_shapes=[pltpu.VMEM((seq + tq, 2 * LANES), bf16),
                        pltpu.VMEM((2, 1, tq), f32), pltpu.VMEM((2, V_ROWS, tq), f32),
                        pltpu.VMEM((2, tq, tq), f32), pltpu.VMEM((2, tq, tq), f32),
                        pltpu.VMEM((2, tq, tq), bf16), pltpu.VMEM((2, tq, tq), bf16),
                        pltpu.VMEM((2, 1, tq), f32), pltpu.VMEM((2, 1, tq), f32)],
        compiler_params=_params(("arbitrary", "arbitrary", "arbitrary")),
        name="flash_attention",
    )(qt, ka, vt, kpad)


def _ffn(hs, hm, h, woa, wob, g, wgu, wd, gf, final):
    tokens = h.shape[0]
    tok = lambda t: (t, 0)
    return pl.pallas_call(
        functools.partial(_ffn_kernel, final=final),
        grid=(tokens // FFN_TILE,),
        in_specs=[pl.BlockSpec((FFN_TILE, SELF_WIDTH), tok),
                  pl.BlockSpec((FFN_TILE, MEM_WIDTH), tok),
                  pl.BlockSpec((FFN_TILE, D_MODEL), tok),
                  _const_spec(woa.shape), _const_spec(wob.shape),
                  _const_spec((1, D_MODEL)),
                  _const_spec(wgu.shape), _const_spec(wd.shape),
                  _const_spec((1, D_MODEL))],
        out_specs=pl.BlockSpec((FFN_TILE, D_MODEL), tok),
        out_shape=jax.ShapeDtypeStruct((tokens, D_MODEL), jnp.float32),
        compiler_params=_params(("arbitrary",)),
        name="out_proj_ffn",
    )(hs, hm, h, woa, wob, g, wgu, wd, gf)


def kernel(x, mem, norm_mix, norm_mem, norm_ffn, norm_final, w_in_fox, b_fgate,
           w_in_moba, w_mem_kv, w_out, w_gate_up, w_down):
    batch, seq, _ = x.shape
    depth = w_out.shape[0]
    bf16 = jnp.bfloat16
    assert seq % (TOKEN_TILE * FLASH_QUERY_TILES) == 0 and seq // MOBA_BLOCK <= MAX_BLOCKS
    assert (batch * seq) % FFN_TILE == 0

    slopes = tuple(2.0 ** (-8.0 * (hd + 1) / N_SELF_HEADS) for hd in range(N_SELF_HEADS))
    h = x.reshape(batch * seq, D_MODEL)
    mkv_all = _mem_kv(mem.reshape(batch * N_MEM, D_MODEL), norm_mem,
                      w_mem_kv.astype(bf16), batch)
    gf = norm_final.reshape(1, D_MODEL)

    for i in range(depth):
        j = i // 2
        gain = norm_mix[i].reshape(1, D_MODEL)
        if i % 2 == 0:
            w = w_in_fox[j]
            f_cols = jnp.pad(w[:, FOX_F_OFF:FOX_F_OFF + N_SELF_HEADS],
                             ((0, 0), (0, LANES - N_SELF_HEADS)))
            w = jnp.concatenate([w[:, :FOX_F_OFF], f_cols,
                                 w[:, FOX_F_OFF + N_SELF_HEADS:]], axis=1).astype(bf16)
            bias = jnp.pad(b_fgate[j], (0, LANES - N_SELF_HEADS)).reshape(1, LANES)
            qt, ka, vt, hm = _proj_call(
                _fox_proj_kernel, "fox_proj", h, gain, w, [bias], mkv_all[i], batch, seq,
                [pltpu.VMEM((8, LANES), jnp.float32)])
        else:
            qt, ka, vt, hm = _proj_call(
                functools.partial(_moba_proj_kernel, slopes=slopes), "moba_proj",
                h, gain, w_in_moba[j].astype(bf16), [], mkv_all[i], batch, seq,
                [pltpu.VMEM((N_PAIRS, 2 * MAX_BLOCKS, LANES), jnp.float32)])
        hs = _flash(qt, ka, vt, batch, seq)
        wo = w_out[i].astype(bf16)
        h = _ffn(hs, hm, h, wo[:SELF_WIDTH], wo[SELF_WIDTH:],
                 norm_ffn[i].reshape(1, D_MODEL), w_gate_up[i].astype(bf16),
                 w_down[i].astype(bf16), gf, final=(i == depth - 1))
    return h.reshape(batch, seq, D_MODEL)
```

```python
import functools

import jax
import jax.numpy as jnp
from jax import lax
from jax.experimental import pallas as pl
from jax.experimental.pallas import tpu as pltpu

D_MODEL = 1024
N_SELF_HEADS = 12
N_MEM_HEADS = 4
HEAD_DIM = 64
SELF_WIDTH = N_SELF_HEADS * HEAD_DIM
MEM_WIDTH = N_MEM_HEADS * HEAD_DIM
N_MEM = 256
D_FF = 2816
MOBA_BLOCK = 256
MOBA_TOPK = 3
RMS_EPS = 1e-6
NEG = -1e30
BELOW_NEG = -3e38

LANES = 128
N_PAIRS = N_SELF_HEADS // 2
AUG_WIDTH = N_SELF_HEADS * LANES
TOKEN_TILE = MOBA_BLOCK
FFN_TILE = 512
FLASH_UNROLLS = (8, 4, 2, 1)
FLASH_QUERY_TILES = 8
MAX_BLOCKS = 32
SEL_ROW = 16
SCALE = HEAD_DIM ** -0.5
LOG2E = 1.4426950408889634
V_ROWS = 80
VMEM_LIMIT = 56 * 1024 * 1024

Q_OFF, K_OFF, V_OFF = 0, SELF_WIDTH, 2 * SELF_WIDTH
FOX_F_OFF = 3 * SELF_WIDTH
FOX_QM_OFF = FOX_F_OFF + LANES
MOBA_QM_OFF = 3 * SELF_WIDTH


def _nt_dot(a, b):
    return lax.dot_general(a, b, (((1,), (1,)), ((), ())),
                           preferred_element_type=jnp.float32)


def _dot(a, b):
    return jnp.dot(a, b, preferred_element_type=jnp.float32)


def _rms(xf, g):
    ms = jnp.mean(xf * xf, axis=-1, keepdims=True)
    return xf * lax.rsqrt(ms + RMS_EPS) * g


def _split3(x):
    hi = x.astype(jnp.bfloat16).astype(jnp.float32)
    r = x - hi
    mid = r.astype(jnp.bfloat16).astype(jnp.float32)
    return hi, mid, r - mid


def _memory_attention(qm, mk_ref, mv_ref, hm_ref, lane):
    for p in range(N_MEM_HEADS // 2):
        blk = qm[:, LANES * p:LANES * (p + 1)]
        mk = mk_ref[:, LANES * p:LANES * (p + 1)]
        mv = mv_ref[:, LANES * p:LANES * (p + 1)]
        outs = []
        for e in range(2):
            in_head = (lane >= HEAD_DIM * e) & (lane < HEAD_DIM * (e + 1))
            qh = jnp.where(in_head, blk, 0.0).astype(jnp.bfloat16)
            s = _nt_dot(qh, mk) * SCALE
            s = s - jnp.max(s, axis=-1, keepdims=True)
            pr = jnp.exp(s)
            den = jnp.sum(pr, axis=-1, keepdims=True)
            outs.append(_dot(pr.astype(jnp.bfloat16), mv) / den)
        out = jnp.where(lane < HEAD_DIM, outs[0], outs[1])
        hm_ref[:, LANES * p:LANES * (p + 1)] = out.astype(hm_ref.dtype)


def _write_vt(vt_ref, vt):
    tm = vt.shape[1]
    pad_rows = V_ROWS - HEAD_DIM
    ones_blk = jnp.where(lax.broadcasted_iota(jnp.int32, (pad_rows, tm), 0) == 0, 1.0, 0.0)
    for hd in range(N_SELF_HEADS):
        vt_ref[0, V_ROWS * hd:V_ROWS * hd + HEAD_DIM, :] = (
            vt[HEAD_DIM * hd:HEAD_DIM * (hd + 1)].astype(vt_ref.dtype))
        vt_ref[0, V_ROWS * hd + HEAD_DIM:V_ROWS * (hd + 1), :] = ones_blk.astype(vt_ref.dtype)


def _fox_proj_kernel(h_ref, g_ref, w_ref, wqvt_ref, b_ref, mk_ref, mv_ref,
                     qt_ref, ka_ref, vt_ref, hm_ref, carry_ref):
    i = pl.program_id(1)
    tm = h_ref.shape[0]
    xn = _rms(h_ref[...], g_ref[...]).astype(jnp.bfloat16)
    lane = lax.broadcasted_iota(jnp.int32, (tm, LANES), 1)

    f_logit = _dot(xn, w_ref[:, FOX_F_OFF:FOX_F_OFF + LANES]) + b_ref[...]
    log_f = jnp.minimum(f_logit, 0.0) - jnp.log(1.0 + jnp.exp(-jnp.abs(f_logit)))
    row = lax.broadcasted_iota(jnp.int32, (tm, tm), 0)
    col = lax.broadcasted_iota(jnp.int32, (tm, tm), 1)
    tri = jnp.where(row >= col, 1.0, 0.0).astype(jnp.bfloat16)
    hi, mid, lo = _split3(log_f)
    cs = _dot(tri, jnp.concatenate([hi, mid, lo], axis=1).astype(jnp.bfloat16))
    local = cs[:, :LANES] + cs[:, LANES:2 * LANES] + cs[:, 2 * LANES:]

    @pl.when(i == 0)
    def _():
        carry_ref[...] = jnp.zeros_like(carry_ref)

    f_cum = local + carry_ref[0:1, :]
    carry_ref[0:1, :] = f_cum[tm - 1:tm, :]
    f_log2 = f_cum * LOG2E
    f_cols = _split3(f_log2)
    f_rows = _split3(f_log2.T)

    qvt = _nt_dot(wqvt_ref[...], xn)
    _write_vt(vt_ref, qvt[SELF_WIDTH:])
    qm = _dot(xn, w_ref[:, FOX_QM_OFF:FOX_QM_OFF + MEM_WIDTH])
    _memory_attention(qm, mk_ref, mv_ref, hm_ref, lane)
    k = _dot(xn, w_ref[:, K_OFF:K_OFF + SELF_WIDTH])
    brow = lax.broadcasted_iota(jnp.int32, (SEL_ROW, tm), 0)
    tail = jnp.zeros((HEAD_DIM - SEL_ROW, tm), jnp.float32)
    for p in range(N_PAIRS):
        kp = k[:, LANES * p:LANES * (p + 1)]
        for e in range(2):
            hd = 2 * p + e
            base = HEAD_DIM * (1 - e)
            bias = jnp.where((brow >= 3) & (brow < 6), 1.0, 0.0)
            ext_k = jnp.where((lane >= base) & (lane < base + 3), 1.0, 0.0)
            for c in range(3):
                bias = jnp.where(brow == c, f_rows[c][hd:hd + 1], bias)
                ext_k = jnp.where(lane == base + 3 + c, -f_cols[c][:, hd:hd + 1], ext_k)
            data = qvt[HEAD_DIM * hd:HEAD_DIM * (hd + 1)] * (SCALE * LOG2E)
            parts = [data, bias, tail] if e == 0 else [bias, tail, data]
            qt_ref[0, LANES * hd:LANES * (hd + 1), :] = (
                jnp.concatenate(parts, axis=0).astype(qt_ref.dtype))
            in_head = (lane >= HEAD_DIM * e) & (lane < HEAD_DIM * (e + 1))
            ka_ref[:, LANES * hd:LANES * (hd + 1)] = (
                jnp.where(in_head, kp, ext_k).astype(ka_ref.dtype))


def _moba_proj_kernel(h_ref, g_ref, w_ref, wqvt_ref, mk_ref, mv_ref,
                      qt_ref, ka_ref, vt_ref, hm_ref, km_ref, *, slopes):
    i = pl.program_id(1)
    tm = h_ref.shape[0]
    xn = _rms(h_ref[...], g_ref[...]).astype(jnp.bfloat16)
    lane = lax.broadcasted_iota(jnp.int32, (tm, LANES), 1)
    lane_row = lax.broadcasted_iota(jnp.int32, (1, LANES), 1)

    qvt = _nt_dot(wqvt_ref[...], xn)
    _write_vt(vt_ref, qvt[SELF_WIDTH:])
    qm = _dot(xn, w_ref[:, MOBA_QM_OFF:MOBA_QM_OFF + MEM_WIDTH])
    _memory_attention(qm, mk_ref, mv_ref, hm_ref, lane)

    @pl.when(i == 0)
    def _():
        km_ref[...] = jnp.zeros_like(km_ref)

    k = _dot(xn, w_ref[:, K_OFF:K_OFF + SELF_WIDTH])
    kmean = jnp.mean(k, axis=0, keepdims=True)

    block_start = (i * tm).astype(jnp.float32)
    offset = lax.broadcasted_iota(jnp.int32, (tm, LANES), 0).astype(jnp.float32)
    ext_k = []
    for e in range(2):
        base = HEAD_DIM * (1 - e)
        x = jnp.where((lane >= base) & (lane < base + 3), 1.0, 0.0)
        x = jnp.where((lane >= base + 4) & (lane < base + 7), block_start, x)
        x = jnp.where((lane >= base + 7) & (lane < base + 10), offset, x)
        ext_k.append(jnp.where(lane == base + SEL_ROW + i, 1.0, x))

    pos_row = (i * tm + lax.broadcasted_iota(jnp.int32, (1, tm), 1)).astype(jnp.float32)
    blk = lax.broadcasted_iota(jnp.int32, (MAX_BLOCKS, tm), 0)
    brow = lax.broadcasted_iota(jnp.int32, (SEL_ROW, tm), 0)
    tail = jnp.zeros((HEAD_DIM - SEL_ROW - MAX_BLOCKS, tm), jnp.float32)

    for p in range(N_PAIRS):
        qt_pair = qvt[LANES * p:LANES * (p + 1)]
        kp = k[:, LANES * p:LANES * (p + 1)]
        q_hi = qt_pair.astype(jnp.bfloat16)
        q_lo = (qt_pair - q_hi.astype(jnp.float32)).astype(jnp.bfloat16)
        km = km_ref[p]
        km_hi = km.astype(jnp.bfloat16)
        km_lo = (km - km_hi.astype(jnp.float32)).astype(jnp.bfloat16)
        r = _dot(jnp.concatenate([km_hi, km_lo], axis=0), q_hi)
        gates = r[:2 * MAX_BLOCKS] + r[2 * MAX_BLOCKS:] + _dot(km_hi, q_lo)
        for e in range(2):
            hd = 2 * p + e
            g = jnp.where(blk < i, gates[MAX_BLOCKS * e:MAX_BLOCKS * (e + 1)], BELOW_NEG)
            chosen = blk == i
            for _ in range(MOBA_TOPK):
                best = jnp.max(g, axis=0, keepdims=True)
                first = jnp.min(jnp.where(g == best, blk, 2 * MAX_BLOCKS),
                                axis=0, keepdims=True)
                hit = blk == first
                chosen = chosen | (hit & (best > 0.5 * BELOW_NEG))
                g = jnp.where(hit, BELOW_NEG, g)
            pen = jnp.where(chosen, 0.0, NEG)

            slope = slopes[hd] * LOG2E
            t_terms = _split3(-slope * pos_row)
            s_terms = _split3(jnp.full((1, tm), slope, jnp.float32))
            bias = jnp.where(brow == 3, 1.0, 0.0)
            for c in range(3):
                bias = jnp.where(brow == c, t_terms[c], bias)
                bias = jnp.where((brow == 4 + c) | (brow == 7 + c), s_terms[c], bias)
            ext_q = jnp.concatenate([bias, pen, tail], axis=0)
            data = qt_pair[HEAD_DIM * e:HEAD_DIM * (e + 1)] * (SCALE * LOG2E)
            parts = [data, ext_q] if e == 0 else [ext_q, data]
            qt_ref[0, LANES * hd:LANES * (hd + 1), :] = (
                jnp.concatenate(parts, axis=0).astype(qt_ref.dtype))

            in_head = (lane >= HEAD_DIM * e) & (lane < HEAD_DIM * (e + 1))
            ka_ref[:, LANES * hd:LANES * (hd + 1)] = (
                jnp.where(in_head, kp, ext_k[e]).astype(ka_ref.dtype))

            in_head_row = (lane_row >= HEAD_DIM * e) & (lane_row < HEAD_DIM * (e + 1))
            km_row = jnp.where(in_head_row, kmean[:, LANES * p:LANES * (p + 1)], 0.0)
            km_ref[p, pl.ds(MAX_BLOCKS * e + i, 1), :] = km_row


def _flash_kernel(qt_ref, ka_ref, vt_ref, kpad_ref, o_ref,
                  kbuf, m_ref, acc_ref, s0, s1, p0, p1, a0, a1):
    n_sub = qt_ref.shape[0]
    tq = qt_ref.shape[2]
    tk = vt_ref.shape[2]
    nq = vt_ref.shape[0]
    seq = ka_ref.shape[0]

    @pl.when(pl.program_id(2) == 0)
    def _():
        kbuf[0:seq, :] = ka_ref[...]
        kbuf[seq:seq + tk, :] = kpad_ref[...]

    for sub in range(n_sub):
        _attend_query_tile(n_sub * pl.program_id(2) + sub, nq, qt_ref.at[sub],
                           o_ref.at[tq * sub:tq * (sub + 1)], vt_ref,
                           kbuf, m_ref, acc_ref, s0, s1, p0, p1, a0, a1)


def _attend_query_tile(i, nq, qt_ref, o_ref, vt_ref, kbuf, m_ref, acc_ref,
                       s0, s1, p0, p1, a0, a1):
    tq = qt_ref.shape[1]
    tk = vt_ref.shape[2]

    def scores(tile, s_ref):
        rows = pl.ds(pl.multiple_of(tile * tk, tk), tk)
        for e in range(2):
            s_ref[e] = _dot(kbuf[rows, LANES * e:LANES * (e + 1)],
                            qt_ref[LANES * e:LANES * (e + 1), :])

    def softmax(s_ref, p_ref, a_ref, causal=False):
        for e in range(2):
            st = s_ref[e]
            if causal:
                kv = lax.broadcasted_iota(jnp.int32, (tk, tq), 0)
                qi = lax.broadcasted_iota(jnp.int32, (tk, tq), 1)
                st = jnp.where(kv <= qi, st, NEG)
            m_prev = m_ref[e]
            m_new = jnp.maximum(m_prev, jnp.max(st, axis=0, keepdims=True))
            a_ref[e] = jnp.exp2(m_prev - m_new)
            p_ref[e] = jnp.exp2(st - m_new).astype(p_ref.dtype)
            m_ref[e] = m_new

    def values(tile, p_ref, a_ref):
        for e in range(2):
            vt = vt_ref[tile, V_ROWS * e:V_ROWS * (e + 1), :]
            acc_ref[e] = a_ref[e] * acc_ref[e] + _dot(vt, p_ref[e])

    def key_tile(pos):
        return jnp.where(pos < i, pos, nq)

    def value_tile(pos):
        return jnp.where(pos < 0, i, jnp.minimum(pos, nq - 1))

    def pipeline_step(h):
        pos = 2 * h
        scores(key_tile(pos + 1), s1)
        values(value_tile(pos - 1), p1, a1)
        softmax(s0, p0, a0)
        scores(key_tile(pos + 2), s0)
        values(value_tile(pos), p0, a0)
        softmax(s1, p1, a1)

    m_ref[...] = jnp.full(m_ref.shape, NEG, jnp.float32)
    acc_ref[...] = jnp.zeros(acc_ref.shape, jnp.float32)
    scores(i, s1)
    scores(key_tile(0), s0)
    softmax(s1, p1, a1, causal=True)
    pipeline_step(0)

    done = 1
    for unroll in FLASH_UNROLLS:
        trips = (i // 2 + 1 - done) // unroll

        def body(t, carry, unroll=unroll, done=done):
            for u in range(unroll):
                pipeline_step(done + unroll * t + u)
            return carry

        lax.fori_loop(0, trips, body, 0)
        done = done + unroll * trips

    outs = []
    for e in range(2):
        acc = acc_ref[e]
        outs.append(acc[:HEAD_DIM] / acc[HEAD_DIM:HEAD_DIM + 1])
    o_ref[...] = jnp.concatenate(outs, axis=0).T.astype(o_ref.dtype)


def _ffn_kernel(hs_ref, hm_ref, h_ref, woa_ref, wob_ref, g_ref, wgu_ref, wd_ref,
                gf_ref, o_ref, *, final):
    h1 = h_ref[...] + _dot(hs_ref[...], woa_ref[...]) + _dot(hm_ref[...], wob_ref[...])
    hn = _rms(h1, g_ref[...]).astype(jnp.bfloat16)
    ffn = None
    for c0, c1 in ((0, 1536), (1536, D_FF)):
        gate = _dot(hn, wgu_ref[:, c0:c1])
        up = _dot(hn, wgu_ref[:, D_FF + c0:D_FF + c1])
        act = (gate * jax.nn.sigmoid(gate) * up).astype(jnp.bfloat16)
        down = _dot(act, wd_ref[c0:c1, :])
        ffn = down if ffn is None else ffn + down
    y = h1 + ffn
    if final:
        y = _rms(y, gf_ref[...])
    o_ref[...] = y


def _mem_kv_kernel(mem_ref, g_ref, w_ref, o_ref):
    mn = _rms(mem_ref[...], g_ref[0]).astype(jnp.bfloat16)
    o_ref[0] = _dot(mn, w_ref[0]).astype(o_ref.dtype)


def _params(sem):
    return pltpu.CompilerParams(dimension_semantics=sem, vmem_limit_bytes=VMEM_LIMIT)


def _const_spec(shape):
    nd = len(shape)
    return pl.BlockSpec(shape, lambda *_: (0,) * nd)


def _mem_kv(mem2, norm_mem, w_mem_kv, batch):
    depth = w_mem_kv.shape[0]
    return pl.pallas_call(
        _mem_kv_kernel,
        grid=(depth, batch),
        in_specs=[pl.BlockSpec((N_MEM, D_MODEL), lambda d, b: (b, 0)),
                  pl.BlockSpec((1, 1, D_MODEL), lambda d, b: (d, 0, 0)),
                  pl.BlockSpec((1, D_MODEL, 2 * MEM_WIDTH), lambda d, b: (d, 0, 0))],
        out_specs=pl.BlockSpec((1, N_MEM, 2 * MEM_WIDTH), lambda d, b: (d, b, 0)),
        out_shape=jax.ShapeDtypeStruct((depth, batch * N_MEM, 2 * MEM_WIDTH), jnp.bfloat16),
        compiler_params=_params(("arbitrary", "arbitrary")),
        name="mem_kv",
    )(mem2, norm_mem.reshape(depth, 1, D_MODEL), w_mem_kv)


def _proj_call(kernel_fn, name, h, gain, w, extra, mkv, batch, seq, scratch):
    wt = jnp.concatenate([w[:, Q_OFF:Q_OFF + SELF_WIDTH],
                          w[:, V_OFF:V_OFF + SELF_WIDTH]], axis=1).T
    nt = seq // TOKEN_TILE
    tokens = batch * seq
    tok = lambda b, i: (b * nt + i, 0)
    tile = lambda b, i: (b * nt + i, 0, 0)
    bf16 = jnp.bfloat16
    in_specs = [pl.BlockSpec((TOKEN_TILE, D_MODEL), tok),
                _const_spec((1, D_MODEL)),
                _const_spec(w.shape),
                _const_spec(wt.shape)]
    args = [h, gain, w, wt]
    for a in extra:
        in_specs.append(_const_spec(a.shape))
        args.append(a)
    in_specs += [pl.BlockSpec((N_MEM, MEM_WIDTH), lambda b, i: (b, 0)),
                 pl.BlockSpec((N_MEM, MEM_WIDTH), lambda b, i: (b, 1))]
    args += [mkv, mkv]
    out_specs = [pl.BlockSpec((1, AUG_WIDTH, TOKEN_TILE), tile),
                 pl.BlockSpec((TOKEN_TILE, AUG_WIDTH), tok),
                 pl.BlockSpec((1, N_SELF_HEADS * V_ROWS, TOKEN_TILE), tile),
                 pl.BlockSpec((TOKEN_TILE, MEM_WIDTH), tok)]
    out_shape = [jax.ShapeDtypeStruct((tokens // TOKEN_TILE, AUG_WIDTH, TOKEN_TILE), bf16),
                 jax.ShapeDtypeStruct((tokens, AUG_WIDTH), bf16),
                 jax.ShapeDtypeStruct((tokens // TOKEN_TILE, N_SELF_HEADS * V_ROWS, TOKEN_TILE),
                                      bf16),
                 jax.ShapeDtypeStruct((tokens, MEM_WIDTH), bf16)]
    return pl.pallas_call(
        kernel_fn,
        grid=(batch, nt),
        in_specs=in_specs,
        out_specs=out_specs,
        out_shape=out_shape,
        scratch_shapes=scratch,
        compiler_params=_params(("arbitrary", "arbitrary")),
        name=name,
    )(*args)


def _flash(qt, ka, vt, batch, seq):
    nq = seq // TOKEN_TILE
    tokens = batch * seq
    tq = TOKEN_TILE
    f32, bf16 = jnp.float32, jnp.bfloat16
    lane = jnp.arange(2 * LANES)
    ones_lane = (lane == HEAD_DIM + 3) | (lane == LANES + 3)
    kpad = jnp.broadcast_to(jnp.where(ones_lane, NEG, 0.0), (tq, 2 * LANES)).astype(bf16)
    n_sub = FLASH_QUERY_TILES
    steps = nq // n_sub
    return pl.pallas_call(
        _flash_kernel,
        grid=(batch, N_PAIRS, steps),
        in_specs=[pl.BlockSpec((n_sub, 2 * LANES, tq), lambda b, p, i: (b * steps + i, p, 0)),
                  pl.BlockSpec((seq, 2 * LANES), lambda b, p, i: (b, p)),
                  pl.BlockSpec((nq, 2 * V_ROWS, tq), lambda b, p, i: (b, p, 0)),
                  _const_spec(kpad.shape)],
        out_specs=pl.BlockSpec((n_sub * tq, LANES), lambda b, p, i: (b * steps + i, p)),
        out_shape=jax.ShapeDtypeStruct((tokens, SELF_WIDTH), bf16),
        scratch_shapes=[pltpu.VMEM((seq + tq, 2 * LANES), bf16),
                        pltpu.VMEM((2, 1, tq), f32), pltpu.VMEM((2, V_ROWS, tq), f32),
                        pltpu.VMEM((2, tq, tq), f32), pltpu.VMEM((2, tq, tq), f32),
                        pltpu.VMEM((2, tq, tq), bf16), pltpu.VMEM((2, tq, tq), bf16),
                        pltpu.VMEM((2, 1, tq), f32), pltpu.VMEM((2, 1, tq), f32)],
        compiler_params=_params(("arbitrary", "arbitrary", "arbitrary")),
        name="flash_attention",
    )(qt, ka, vt, kpad)


def _ffn(hs, hm, h, woa, wob, g, wgu, wd, gf, final):
    tokens = h.shape[0]
    tok = lambda t: (t, 0)
    return pl.pallas_call(
        functools.partial(_ffn_kernel, final=final),
        grid=(tokens // FFN_TILE,),
        in_specs=[pl.BlockSpec((FFN_TILE, SELF_WIDTH), tok),
                  pl.BlockSpec((FFN_TILE, MEM_WIDTH), tok),
                  pl.BlockSpec((FFN_TILE, D_MODEL), tok),
                  _const_spec(woa.shape), _const_spec(wob.shape),
                  _const_spec((1, D_MODEL)),
                  _const_spec(wgu.shape), _const_spec(wd.shape),
                  _const_spec((1, D_MODEL))],
        out_specs=pl.BlockSpec((FFN_TILE, D_MODEL), tok),
        out_shape=jax.ShapeDtypeStruct((tokens, D_MODEL), jnp.float32),
        compiler_params=_params(("arbitrary",)),
        name="out_proj_ffn",
    )(hs, hm, h, woa, wob, g, wgu, wd, gf)


def kernel(x, mem, norm_mix, norm_mem, norm_ffn, norm_final, w_in_fox, b_fgate,
           w_in_moba, w_mem_kv, w_out, w_gate_up, w_down):
    batch, seq, _ = x.shape
    depth = w_out.shape[0]
    bf16 = jnp.bfloat16
    assert seq % (TOKEN_TILE * FLASH_QUERY_TILES) == 0 and seq // MOBA_BLOCK <= MAX_BLOCKS
    assert (batch * seq) % FFN_TILE == 0

    slopes = tuple(2.0 ** (-8.0 * (hd + 1) / N_SELF_HEADS) for hd in range(N_SELF_HEADS))
    h = x.reshape(batch * seq, D_MODEL)
    mkv_all = _mem_kv(mem.reshape(batch * N_MEM, D_MODEL), norm_mem,
                      w_mem_kv.astype(bf16), batch)
    gf = norm_final.reshape(1, D_MODEL)

    for i in range(depth):
        j = i // 2
        gain = norm_mix[i].reshape(1, D_MODEL)
        if i % 2 == 0:
            w = w_in_fox[j]
            f_cols = jnp.pad(w[:, FOX_F_OFF:FOX_F_OFF + N_SELF_HEADS],
                             ((0, 0), (0, LANES - N_SELF_HEADS)))
            w = jnp.concatenate([w[:, :FOX_F_OFF], f_cols,
                                 w[:, FOX_F_OFF + N_SELF_HEADS:]], axis=1).astype(bf16)
            bias = jnp.pad(b_fgate[j], (0, LANES - N_SELF_HEADS)).reshape(1, LANES)
            qt, ka, vt, hm = _proj_call(
                _fox_proj_kernel, "fox_proj", h, gain, w, [bias], mkv_all[i], batch, seq,
                [pltpu.VMEM((8, LANES), jnp.float32)])
        else:
            qt, ka, vt, hm = _proj_call(
                functools.partial(_moba_proj_kernel, slopes=slopes), "moba_proj",
                h, gain, w_in_moba[j].astype(bf16), [], mkv_all[i], batch, seq,
                [pltpu.VMEM((N_PAIRS, 2 * MAX_BLOCKS, LANES), jnp.float32)])
        hs = _flash(qt, ka, vt, batch, seq)
        wo = w_out[i].astype(bf16)
        h = _ffn(hs, hm, h, wo[:SELF_WIDTH], wo[SELF_WIDTH:],
                 norm_ffn[i].reshape(1, D_MODEL), w_gate_up[i].astype(bf16),
                 w_down[i].astype(bf16), gf, final=(i == depth - 1))
    return h.reshape(batch, seq, D_MODEL)
```

```python
import functools

import jax
import jax.numpy as jnp
from jax import lax
from jax.experimental import pallas as pl
from jax.experimental.pallas import tpu as pltpu

D_MODEL = 1024
N_SELF_HEADS = 12
N_MEM_HEADS = 4
HEAD_DIM = 64
SELF_WIDTH = N_SELF_HEADS * HEAD_DIM
MEM_WIDTH = N_MEM_HEADS * HEAD_DIM
N_MEM = 256
D_FF = 2816
MOBA_BLOCK = 256
MOBA_TOPK = 3
RMS_EPS = 1e-6
NEG = -1e30
BELOW_NEG = -3e38

LANES = 128
N_PAIRS = N_SELF_HEADS // 2
AUG_WIDTH = N_SELF_HEADS * LANES
TOKEN_TILE = MOBA_BLOCK
FFN_TILE = 512
FLASH_UNROLLS = (8, 4, 2, 1)
FLASH_QUERY_TILES = 4
MAX_BLOCKS = 32
SEL_ROW = 16
SCALE = HEAD_DIM ** -0.5
LOG2E = 1.4426950408889634
V_ROWS = 80
VMEM_LIMIT = 56 * 1024 * 1024

Q_OFF, K_OFF, V_OFF = 0, SELF_WIDTH, 2 * SELF_WIDTH
FOX_F_OFF = 3 * SELF_WIDTH
FOX_QM_OFF = FOX_F_OFF + LANES
MOBA_QM_OFF = 3 * SELF_WIDTH


def _nt_dot(a, b):
    return lax.dot_general(a, b, (((1,), (1,)), ((), ())),
                           preferred_element_type=jnp.float32)


def _dot(a, b):
    return jnp.dot(a, b, preferred_element_type=jnp.float32)


def _rms(xf, g):
    ms = jnp.mean(xf * xf, axis=-1, keepdims=True)
    return xf * lax.rsqrt(ms + RMS_EPS) * g


def _split3(x):
    hi = x.astype(jnp.bfloat16).astype(jnp.float32)
    r = x - hi
    mid = r.astype(jnp.bfloat16).astype(jnp.float32)
    return hi, mid, r - mid


def _memory_attention(qm, mk_ref, mv_ref, hm_ref, lane):
    for p in range(N_MEM_HEADS // 2):
        blk = qm[:, LANES * p:LANES * (p + 1)]
        mk = mk_ref[:, LANES * p:LANES * (p + 1)]
        mv = mv_ref[:, LANES * p:LANES * (p + 1)]
        outs = []
        for e in range(2):
            in_head = (lane >= HEAD_DIM * e) & (lane < HEAD_DIM * (e + 1))
            qh = jnp.where(in_head, blk, 0.0).astype(jnp.bfloat16)
            s = _nt_dot(qh, mk) * SCALE
            s = s - jnp.max(s, axis=-1, keepdims=True)
            pr = jnp.exp(s)
            den = jnp.sum(pr, axis=-1, keepdims=True)
            outs.append(_dot(pr.astype(jnp.bfloat16), mv) / den)
        out = jnp.where(lane < HEAD_DIM, outs[0], outs[1])
        hm_ref[:, LANES * p:LANES * (p + 1)] = out.astype(hm_ref.dtype)


def _write_vt(vt_ref, vt):
    tm = vt.shape[1]
    pad_rows = V_ROWS - HEAD_DIM
    ones_blk = jnp.where(lax.broadcasted_iota(jnp.int32, (pad_rows, tm), 0) == 0, 1.0, 0.0)
    for hd in range(N_SELF_HEADS):
        vt_ref[0, V_ROWS * hd:V_ROWS * hd + HEAD_DIM, :] = (
            vt[HEAD_DIM * hd:HEAD_DIM * (hd + 1)].astype(vt_ref.dtype))
        vt_ref[0, V_ROWS * hd + HEAD_DIM:V_ROWS * (hd + 1), :] = ones_blk.astype(vt_ref.dtype)


def _fox_proj_kernel(h_ref, g_ref, w_ref, wqvt_ref, b_ref, mk_ref, mv_ref,
                     qt_ref, ka_ref, vt_ref, hm_ref, carry_ref):
    i = pl.program_id(1)
    tm = h_ref.shape[0]
    xn = _rms(h_ref[...], g_ref[...]).astype(jnp.bfloat16)
    lane = lax.broadcasted_iota(jnp.int32, (tm, LANES), 1)

    f_logit = _dot(xn, w_ref[:, FOX_F_OFF:FOX_F_OFF + LANES]) + b_ref[...]
    log_f = jnp.minimum(f_logit, 0.0) - jnp.log(1.0 + jnp.exp(-jnp.abs(f_logit)))
    row = lax.broadcasted_iota(jnp.int32, (tm, tm), 0)
    col = lax.broadcasted_iota(jnp.int32, (tm, tm), 1)
    tri = jnp.where(row >= col, 1.0, 0.0).astype(jnp.bfloat16)
    hi, mid, lo = _split3(log_f)
    cs = _dot(tri, jnp.concatenate([hi, mid, lo], axis=1).astype(jnp.bfloat16))
    local = cs[:, :LANES] + cs[:, LANES:2 * LANES] + cs[:, 2 * LANES:]

    @pl.when(i == 0)
    def _():
        carry_ref[...] = jnp.zeros_like(carry_ref)

    f_cum = local + carry_ref[0:1, :]
    carry_ref[0:1, :] = f_cum[tm - 1:tm, :]
    f_log2 = f_cum * LOG2E
    f_cols = _split3(f_log2)
    f_rows = _split3(f_log2.T)

    qvt = _nt_dot(wqvt_ref[...], xn)
    _write_vt(vt_ref, qvt[SELF_WIDTH:])
    qm = _dot(xn, w_ref[:, FOX_QM_OFF:FOX_QM_OFF + MEM_WIDTH])
    _memory_attention(qm, mk_ref, mv_ref, hm_ref, lane)
    k = _dot(xn, w_ref[:, K_OFF:K_OFF + SELF_WIDTH])
    brow = lax.broadcasted_iota(jnp.int32, (SEL_ROW, tm), 0)
    tail = jnp.zeros((HEAD_DIM - SEL_ROW, tm), jnp.float32)
    for p in range(N_PAIRS):
        kp = k[:, LANES * p:LANES * (p + 1)]
        for e in range(2):
            hd = 2 * p + e
            base = HEAD_DIM * (1 - e)
            bias = jnp.where((brow >= 3) & (brow < 6), 1.0, 0.0)
            ext_k = jnp.where((lane >= base) & (lane < base + 3), 1.0, 0.0)
            for c in range(3):
                bias = jnp.where(brow == c, f_rows[c][hd:hd + 1], bias)
                ext_k = jnp.where(lane == base + 3 + c, -f_cols[c][:, hd:hd + 1], ext_k)
            data = qvt[HEAD_DIM * hd:HEAD_DIM * (hd + 1)] * (SCALE * LOG2E)
            parts = [data, bias, tail] if e == 0 else [bias, tail, data]
            qt_ref[0, LANES * hd:LANES * (hd + 1), :] = (
                jnp.concatenate(parts, axis=0).astype(qt_ref.dtype))
            in_head = (lane >= HEAD_DIM * e) & (lane < HEAD_DIM * (e + 1))
            ka_ref[:, LANES * hd:LANES * (hd + 1)] = (
                jnp.where(in_head, kp, ext_k).astype(ka_ref.dtype))


def _moba_proj_kernel(h_ref, g_ref, w_ref, wqvt_ref, mk_ref, mv_ref,
                      qt_ref, ka_ref, vt_ref, hm_ref, km_ref, *, slopes):
    i = pl.program_id(1)
    tm = h_ref.shape[0]
    xn = _rms(h_ref[...], g_ref[...]).astype(jnp.bfloat16)
    lane = lax.broadcasted_iota(jnp.int32, (tm, LANES), 1)
    lane_row = lax.broadcasted_iota(jnp.int32, (1, LANES), 1)

    qvt = _nt_dot(wqvt_ref[...], xn)
    _write_vt(vt_ref, qvt[SELF_WIDTH:])
    qm = _dot(xn, w_ref[:, MOBA_QM_OFF:MOBA_QM_OFF + MEM_WIDTH])
    _memory_attention(qm, mk_ref, mv_ref, hm_ref, lane)

    @pl.when(i == 0)
    def _():
        km_ref[...] = jnp.zeros_like(km_ref)

    k = _dot(xn, w_ref[:, K_OFF:K_OFF + SELF_WIDTH])
    kmean = jnp.mean(k, axis=0, keepdims=True)

    block_start = (i * tm).astype(jnp.float32)
    offset = lax.broadcasted_iota(jnp.int32, (tm, LANES), 0).astype(jnp.float32)
    ext_k = []
    for e in range(2):
        base = HEAD_DIM * (1 - e)
        x = jnp.where((lane >= base) & (lane < base + 3), 1.0, 0.0)
        x = jnp.where((lane >= base + 4) & (lane < base + 7), block_start, x)
        x = jnp.where((lane >= base + 7) & (lane < base + 10), offset, x)
        ext_k.append(jnp.where(lane == base + SEL_ROW + i, 1.0, x))

    pos_row = (i * tm + lax.broadcasted_iota(jnp.int32, (1, tm), 1)).astype(jnp.float32)
    blk = lax.broadcasted_iota(jnp.int32, (MAX_BLOCKS, tm), 0)
    brow = lax.broadcasted_iota(jnp.int32, (SEL_ROW, tm), 0)
    tail = jnp.zeros((HEAD_DIM - SEL_ROW - MAX_BLOCKS, tm), jnp.float32)

    for p in range(N_PAIRS):
        qt_pair = qvt[LANES * p:LANES * (p + 1)]
        kp = k[:, LANES * p:LANES * (p + 1)]
        q_hi = qt_pair.astype(jnp.bfloat16)
        q_lo = (qt_pair - q_hi.astype(jnp.float32)).astype(jnp.bfloat16)
        km = km_ref[p]
        km_hi = km.astype(jnp.bfloat16)
        km_lo = (km - km_hi.astype(jnp.float32)).astype(jnp.bfloat16)
        r = _dot(jnp.concatenate([km_hi, km_lo], axis=0), q_hi)
        gates = r[:2 * MAX_BLOCKS] + r[2 * MAX_BLOCKS:] + _dot(km_hi, q_lo)
        for e in range(2):
            hd = 2 * p + e
            g = jnp.where(blk < i, gates[MAX_BLOCKS * e:MAX_BLOCKS * (e + 1)], BELOW_NEG)
            chosen = blk == i
            for _ in range(MOBA_TOPK):
                best = jnp.max(g, axis=0, keepdims=True)
                first = jnp.min(jnp.where(g == best, blk, 2 * MAX_BLOCKS),
                                axis=0, keepdims=True)
                hit = blk == first
                chosen = chosen | (hit & (best > 0.5 * BELOW_NEG))
                g = jnp.where(hit, BELOW_NEG, g)
            pen = jnp.where(chosen, 0.0, NEG)

            slope = slopes[hd] * LOG2E
            t_terms = _split3(-slope * pos_row)
            s_terms = _split3(jnp.full((1, tm), slope, jnp.float32))
            bias = jnp.where(brow == 3, 1.0, 0.0)
            for c in range(3):
                bias = jnp.where(brow == c, t_terms[c], bias)
                bias = jnp.where((brow == 4 + c) | (brow == 7 + c), s_terms[c], bias)
            ext_q = jnp.concatenate([bias, pen, tail], axis=0)
            data = qt_pair[HEAD_DIM * e:HEAD_DIM * (e + 1)] * (SCALE * LOG2E)
            parts = [data, ext_q] if e == 0 else [ext_q, data]
            qt_ref[0, LANES * hd:LANES * (hd + 1), :] = (
                jnp.concatenate(parts, axis=0).astype(qt_ref.dtype))

            in_head = (lane >= HEAD_DIM * e) & (lane < HEAD_DIM * (e + 1))
            ka_ref[:, LANES * hd:LANES * (hd + 1)] = (
                jnp.where(in_head, kp, ext_k[e]).astype(ka_ref.dtype))

            in_head_row = (lane_row >= HEAD_DIM * e) & (lane_row < HEAD_DIM * (e + 1))
            km_row = jnp.where(in_head_row, kmean[:, LANES * p:LANES * (p + 1)], 0.0)
            km_ref[p, pl.ds(MAX_BLOCKS * e + i, 1), :] = km_row


def _flash_kernel(qt_ref, ka_ref, vt_ref, kpad_ref, o_ref,
                  kbuf, m_ref, acc_ref, s0, s1, p0, p1, a0, a1):
    n_sub = qt_ref.shape[0]
    tq = qt_ref.shape[2]
    tk = vt_ref.shape[2]
    nq = vt_ref.shape[0]
    seq = ka_ref.shape[0]

    @pl.when(pl.program_id(2) == 0)
    def _():
        kbuf[0:seq, :] = ka_ref[...]
        kbuf[seq:seq + tk, :] = kpad_ref[...]

    for sub in range(n_sub):
        _attend_query_tile(n_sub * pl.program_id(2) + sub, nq, qt_ref.at[sub],
                           o_ref.at[tq * sub:tq * (sub + 1)], vt_ref,
                           kbuf, m_ref, acc_ref, s0, s1, p0, p1, a0, a1)


def _attend_query_tile(i, nq, qt_ref, o_ref, vt_ref, kbuf, m_ref, acc_ref,
                       s0, s1, p0, p1, a0, a1):
    tq = qt_ref.shape[1]
    tk = vt_ref.shape[2]

    def scores(tile, s_ref):
        rows = pl.ds(pl.multiple_of(tile * tk, tk), tk)
        for e in range(2):
            st = _dot(kbuf[rows, LANES * e:LANES * (e + 1)],
                      qt_ref[LANES * e:LANES * (e + 1), :])
            for c in range(tq // LANES):
                s_ref[e, c] = st[:, LANES * c:LANES * (c + 1)]

    def softmax(s_ref, p_ref, a_ref, causal=False):
        for e in range(2):
            st = jnp.concatenate([s_ref[e, c] for c in range(tq // LANES)], axis=1)
            if causal:
                kv = lax.broadcasted_iota(jnp.int32, (tk, tq), 0)
                qi = lax.broadcasted_iota(jnp.int32, (tk, tq), 1)
                st = jnp.where(kv <= qi, st, NEG)
            m_prev = m_ref[e]
            m_new = jnp.maximum(m_prev, jnp.max(st, axis=0, keepdims=True))
            a_ref[e] = jnp.exp2(m_prev - m_new)
            pt = jnp.exp2(st - m_new).astype(p_ref.dtype)
            for c in range(tq // LANES):
                p_ref[e, c] = pt[:, LANES * c:LANES * (c + 1)]
            m_ref[e] = m_new

    def values(tile, p_ref, a_ref):
        for e in range(2):
            vt = vt_ref[tile, V_ROWS * e:V_ROWS * (e + 1), :]
            pt = jnp.concatenate([p_ref[e, c] for c in range(tq // LANES)], axis=1)
            acc_ref[e] = a_ref[e] * acc_ref[e] + _dot(vt, pt)

    def key_tile(pos):
        return jnp.where(pos < i, pos, nq)

    def value_tile(pos):
        return jnp.where(pos < 0, i, jnp.minimum(pos, nq - 1))

    def pipeline_step(h):
        pos = 2 * h
        scores(key_tile(pos + 1), s1)
        values(value_tile(pos - 1), p1, a1)
        softmax(s0, p0, a0)
        scores(key_tile(pos + 2), s0)
        values(value_tile(pos), p0, a0)
        softmax(s1, p1, a1)

    m_ref[...] = jnp.full(m_ref.shape, NEG, jnp.float32)
    acc_ref[...] = jnp.zeros(acc_ref.shape, jnp.float32)
    scores(i, s1)
    scores(key_tile(0), s0)
    softmax(s1, p1, a1, causal=True)
    pipeline_step(0)

    done = 1
    for unroll in FLASH_UNROLLS:
        trips = (i // 2 + 1 - done) // unroll

        def body(t, carry, unroll=unroll, done=done):
            for u in range(unroll):
                pipeline_step(done + unroll * t + u)
            return carry

        lax.fori_loop(0, trips, body, 0)
        done = done + unroll * trips

    outs = []
    for e in range(2):
        acc = acc_ref[e]
        outs.append(acc[:HEAD_DIM] / acc[HEAD_DIM:HEAD_DIM + 1])
    o_ref[...] = jnp.concatenate(outs, axis=0).T.astype(o_ref.dtype)


def _ffn_kernel(hs_ref, hm_ref, h_ref, woa_ref, wob_ref, g_ref, wgu_ref, wd_ref,
                gf_ref, o_ref, *, final):
    h1 = h_ref[...] + _dot(hs_ref[...], woa_ref[...]) + _dot(hm_ref[...], wob_ref[...])
    hn = _rms(h1, g_ref[...]).astype(jnp.bfloat16)
    ffn = None
    for c0, c1 in ((0, 1536), (1536, D_FF)):
        gate = _dot(hn, wgu_ref[:, c0:c1])
        up = _dot(hn, wgu_ref[:, D_FF + c0:D_FF + c1])
        act = (gate * jax.nn.sigmoid(gate) * up).astype(jnp.bfloat16)
        down = _dot(act, wd_ref[c0:c1, :])
        ffn = down if ffn is None else ffn + down
    y = h1 + ffn
    if final:
        y = _rms(y, gf_ref[...])
    o_ref[...] = y


def _mem_kv_kernel(mem_ref, g_ref, w_ref, o_ref):
    mn = _rms(mem_ref[...], g_ref[0]).astype(jnp.bfloat16)
    o_ref[0] = _dot(mn, w_ref[0]).astype(o_ref.dtype)


def _params(sem):
    return pltpu.CompilerParams(dimension_semantics=sem, vmem_limit_bytes=VMEM_LIMIT)


def _const_spec(shape):
    nd = len(shape)
    return pl.BlockSpec(shape, lambda *_: (0,) * nd)


def _mem_kv(mem2, norm_mem, w_mem_kv, batch):
    depth = w_mem_kv.shape[0]
    return pl.pallas_call(
        _mem_kv_kernel,
        grid=(depth, batch),
        in_specs=[pl.BlockSpec((N_MEM, D_MODEL), lambda d, b: (b, 0)),
                  pl.BlockSpec((1, 1, D_MODEL), lambda d, b: (d, 0, 0)),
                  pl.BlockSpec((1, D_MODEL, 2 * MEM_WIDTH), lambda d, b: (d, 0, 0))],
        out_specs=pl.BlockSpec((1, N_MEM, 2 * MEM_WIDTH), lambda d, b: (d, b, 0)),
        out_shape=jax.ShapeDtypeStruct((depth, batch * N_MEM, 2 * MEM_WIDTH), jnp.bfloat16),
        compiler_params=_params(("arbitrary", "arbitrary")),
        name="mem_kv",
    )(mem2, norm_mem.reshape(depth, 1, D_MODEL), w_mem_kv)


def _proj_call(kernel_fn, name, h, gain, w, extra, mkv, batch, seq, scratch):
    wt = jnp.concatenate([w[:, Q_OFF:Q_OFF + SELF_WIDTH],
                          w[:, V_OFF:V_OFF + SELF_WIDTH]], axis=1).T
    nt = seq // TOKEN_TILE
    tokens = batch * seq
    tok = lambda b, i: (b * nt + i, 0)
    tile = lambda b, i: (b * nt + i, 0, 0)
    bf16 = jnp.bfloat16
    in_specs = [pl.BlockSpec((TOKEN_TILE, D_MODEL), tok),
                _const_spec((1, D_MODEL)),
                _const_spec(w.shape),
                _const_spec(wt.shape)]
    args = [h, gain, w, wt]
    for a in extra:
        in_specs.append(_const_spec(a.shape))
        args.append(a)
    in_specs += [pl.BlockSpec((N_MEM, MEM_WIDTH), lambda b, i: (b, 0)),
                 pl.BlockSpec((N_MEM, MEM_WIDTH), lambda b, i: (b, 1))]
    args += [mkv, mkv]
    out_specs = [pl.BlockSpec((1, AUG_WIDTH, TOKEN_TILE), tile),
                 pl.BlockSpec((TOKEN_TILE, AUG_WIDTH), tok),
                 pl.BlockSpec((1, N_SELF_HEADS * V_ROWS, TOKEN_TILE), tile),
                 pl.BlockSpec((TOKEN_TILE, MEM_WIDTH), tok)]
    out_shape = [jax.ShapeDtypeStruct((tokens // TOKEN_TILE, AUG_WIDTH, TOKEN_TILE), bf16),
                 jax.ShapeDtypeStruct((tokens, AUG_WIDTH), bf16),
                 jax.ShapeDtypeStruct((tokens // TOKEN_TILE, N_SELF_HEADS * V_ROWS, TOKEN_TILE),
                                      bf16),
                 jax.ShapeDtypeStruct((tokens, MEM_WIDTH), bf16)]
    return pl.pallas_call(
        kernel_fn,
        grid=(batch, nt),
        in_specs=in_specs,
        out_specs=out_specs,
        out_shape=out_shape,
        scratch_shapes=scratch,
        compiler_params=_params(("arbitrary", "arbitrary")),
        name=name,
    )(*args)


def _flash(qt, ka, vt, batch, seq):
    nq = seq // TOKEN_TILE
    tokens = batch * seq
    tq = TOKEN_TILE
    f32, bf16 = jnp.float32, jnp.bfloat16
    lane = jnp.arange(2 * LANES)
    ones_lane = (lane == HEAD_DIM + 3) | (lane == LANES + 3)
    kpad = jnp.broadcast_to(jnp.where(ones_lane, NEG, 0.0), (tq, 2 * LANES)).astype(bf16)
    n_sub = FLASH_QUERY_TILES
    steps = nq // n_sub
    return pl.pallas_call(
        _flash_kernel,
        grid=(batch, N_PAIRS, steps),
        in_specs=[pl.BlockSpec((n_sub, 2 * LANES, tq), lambda b, p, i: (b * steps + i, p, 0)),
                  pl.BlockSpec((seq, 2 * LANES), lambda b, p, i: (b, p)),
                  pl.BlockSpec((nq, 2 * V_ROWS, tq), lambda b, p, i: (b, p, 0)),
                  _const_spec(kpad.shape)],
        out_specs=pl.BlockSpec((n_sub * tq, LANES), lambda b, p, i: (b * steps + i, p)),
        out_shape=jax.ShapeDtypeStruct((tokens, SELF_WIDTH), bf16),
        scratch_shapes=[pltpu.VMEM((seq + tq, 2 * LANES), bf16),
                        pltpu.VMEM((2, 1, tq), f32), pltpu.VMEM((2, V_ROWS, tq), f32),
                        pltpu.VMEM((2, tq // LANES, tq, LANES), f32),
                        pltpu.VMEM((2, tq // LANES, tq, LANES), f32),
                        pltpu.VMEM((2, tq // LANES, tq, LANES), bf16),
                        pltpu.VMEM((2, tq // LANES, tq, LANES), bf16),
                        pltpu.VMEM((2, 1, tq), f32), pltpu.VMEM((2, 1, tq), f32)],
        compiler_params=_params(("arbitrary", "arbitrary", "arbitrary")),
        name="flash_attention",
    )(qt, ka, vt, kpad)


def _ffn(hs, hm, h, woa, wob, g, wgu, wd, gf, final):
    tokens = h.shape[0]
    tok = lambda t: (t, 0)
    return pl.pallas_call(
        functools.partial(_ffn_kernel, final=final),
        grid=(tokens // FFN_TILE,),
        in_specs=[pl.BlockSpec((FFN_TILE, SELF_WIDTH), tok),
                  pl.BlockSpec((FFN_TILE, MEM_WIDTH), tok),
                  pl.BlockSpec((FFN_TILE, D_MODEL), tok),
                  _const_spec(woa.shape), _const_spec(wob.shape),
                  _const_spec((1, D_MODEL)),
                  _const_spec(wgu.shape), _const_spec(wd.shape),
                  _const_spec((1, D_MODEL))],
        out_specs=pl.BlockSpec((FFN_TILE, D_MODEL), tok),
        out_shape=jax.ShapeDtypeStruct((tokens, D_MODEL), jnp.float32),
        compiler_params=_params(("arbitrary",)),
        name="out_proj_ffn",
    )(hs, hm, h, woa, wob, g, wgu, wd, gf)


def kernel(x, mem, norm_mix, norm_mem, norm_ffn, norm_final, w_in_fox, b_fgate,
           w_in_moba, w_mem_kv, w_out, w_gate_up, w_down):
    batch, seq, _ = x.shape
    depth = w_out.shape[0]
    bf16 = jnp.bfloat16
    assert seq % (TOKEN_TILE * FLASH_QUERY_TILES) == 0 and seq // MOBA_BLOCK <= MAX_BLOCKS
    assert (batch * seq) % FFN_TILE == 0

    slopes = tuple(2.0 ** (-8.0 * (hd + 1) / N_SELF_HEADS) for hd in range(N_SELF_HEADS))
    h = x.reshape(batch * seq, D_MODEL)
    mkv_all = _mem_kv(mem.reshape(batch * N_MEM, D_MODEL), norm_mem,
                      w_mem_kv.astype(bf16), batch)
    gf = norm_final.reshape(1, D_MODEL)

    for i in range(depth):
        j = i // 2
        gain = norm_mix[i].reshape(1, D_MODEL)
        if i % 2 == 0:
            w = w_in_fox[j]
            f_cols = jnp.pad(w[:, FOX_F_OFF:FOX_F_OFF + N_SELF_HEADS],
                             ((0, 0), (0, LANES - N_SELF_HEADS)))
            w = jnp.concatenate([w[:, :FOX_F_OFF], f_cols,
                                 w[:, FOX_F_OFF + N_SELF_HEADS:]], axis=1).astype(bf16)
            bias = jnp.pad(b_fgate[j], (0, LANES - N_SELF_HEADS)).reshape(1, LANES)
            qt, ka, vt, hm = _proj_call(
                _fox_proj_kernel, "fox_proj", h, gain, w, [bias], mkv_all[i], batch, seq,
                [pltpu.VMEM((8, LANES), jnp.float32)])
        else:
            qt, ka, vt, hm = _proj_call(
                functools.partial(_moba_proj_kernel, slopes=slopes), "moba_proj",
                h, gain, w_in_moba[j].astype(bf16), [], mkv_all[i], batch, seq,
                [pltpu.VMEM((N_PAIRS, 2 * MAX_BLOCKS, LANES), jnp.float32)])
        hs = _flash(qt, ka, vt, batch, seq)
        wo = w_out[i].astype(bf16)
        h = _ffn(hs, hm, h, wo[:SELF_WIDTH], wo[SELF_WIDTH:],
                 norm_ffn[i].reshape(1, D_MODEL), w_gate_up[i].astype(bf16),
                 w_down[i].astype(bf16), gf, final=(i == depth - 1))
    return h.reshape(batch, seq, D_MODEL)
```

```python
import functools

import jax
import jax.numpy as jnp
from jax import lax
from jax.experimental import pallas as pl
from jax.experimental.pallas import tpu as pltpu

D_MODEL = 1024
N_SELF_HEADS = 12
N_MEM_HEADS = 4
HEAD_DIM = 64
SELF_WIDTH = N_SELF_HEADS * HEAD_DIM
MEM_WIDTH = N_MEM_HEADS * HEAD_DIM
N_MEM = 256
D_FF = 2816
MOBA_BLOCK = 256
MOBA_TOPK = 3
RMS_EPS = 1e-6
NEG = -1e30
BELOW_NEG = -3e38

LANES = 128
N_PAIRS = N_SELF_HEADS // 2
AUG_WIDTH = N_SELF_HEADS * LANES
TOKEN_TILE = MOBA_BLOCK
FFN_TILE = 512
FLASH_UNROLLS = (8, 4, 2, 1)
FLASH_QUERY_TILES = 4
MAX_BLOCKS = 32
SEL_ROW = 16
SCALE = HEAD_DIM ** -0.5
LOG2E = 1.4426950408889634
V_ROWS = 80
VMEM_LIMIT = 56 * 1024 * 1024

Q_OFF, K_OFF, V_OFF = 0, SELF_WIDTH, 2 * SELF_WIDTH
FOX_F_OFF = 3 * SELF_WIDTH
FOX_QM_OFF = FOX_F_OFF + LANES
MOBA_QM_OFF = 3 * SELF_WIDTH


def _nt_dot(a, b):
    return lax.dot_general(a, b, (((1,), (1,)), ((), ())),
                           preferred_element_type=jnp.float32)


def _dot(a, b):
    return jnp.dot(a, b, preferred_element_type=jnp.float32)


def _rms(xf, g):
    ms = jnp.mean(xf * xf, axis=-1, keepdims=True)
    return xf * lax.rsqrt(ms + RMS_EPS) * g


def _split3(x):
    hi = x.astype(jnp.bfloat16).astype(jnp.float32)
    r = x - hi
    mid = r.astype(jnp.bfloat16).astype(jnp.float32)
    return hi, mid, r - mid


def _memory_attention(qm, mk_ref, mv_ref, hm_ref, lane):
    for p in range(N_MEM_HEADS // 2):
        blk = qm[:, LANES * p:LANES * (p + 1)]
        mk = mk_ref[:, LANES * p:LANES * (p + 1)]
        mv = mv_ref[:, LANES * p:LANES * (p + 1)]
        outs = []
        for e in range(2):
            in_head = (lane >= HEAD_DIM * e) & (lane < HEAD_DIM * (e + 1))
            qh = jnp.where(in_head, blk, 0.0).astype(jnp.bfloat16)
            s = _nt_dot(qh, mk) * SCALE
            s = s - jnp.max(s, axis=-1, keepdims=True)
            pr = jnp.exp(s)
            den = jnp.sum(pr, axis=-1, keepdims=True)
            outs.append(_dot(pr.astype(jnp.bfloat16), mv) / den)
        out = jnp.where(lane < HEAD_DIM, outs[0], outs[1])
        hm_ref[:, LANES * p:LANES * (p + 1)] = out.astype(hm_ref.dtype)


def _write_vt(vt_ref, vt):
    tm = vt.shape[1]
    pad_rows = V_ROWS - HEAD_DIM
    ones_blk = jnp.where(lax.broadcasted_iota(jnp.int32, (pad_rows, tm), 0) == 0, 1.0, 0.0)
    for hd in range(N_SELF_HEADS):
        vt_ref[0, V_ROWS * hd:V_ROWS * hd + HEAD_DIM, :] = (
            vt[HEAD_DIM * hd:HEAD_DIM * (hd + 1)].astype(vt_ref.dtype))
        vt_ref[0, V_ROWS * hd + HEAD_DIM:V_ROWS * (hd + 1), :] = ones_blk.astype(vt_ref.dtype)


def _fox_proj_kernel(h_ref, g_ref, w_ref, wqvt_ref, b_ref, mk_ref, mv_ref,
                     qt_ref, ka_ref, vt_ref, hm_ref, carry_ref):
    i = pl.program_id(1)
    tm = h_ref.shape[0]
    xn = _rms(h_ref[...], g_ref[...]).astype(jnp.bfloat16)
    lane = lax.broadcasted_iota(jnp.int32, (tm, LANES), 1)

    f_logit = _dot(xn, w_ref[:, FOX_F_OFF:FOX_F_OFF + LANES]) + b_ref[...]
    log_f = jnp.minimum(f_logit, 0.0) - jnp.log(1.0 + jnp.exp(-jnp.abs(f_logit)))
    row = lax.broadcasted_iota(jnp.int32, (tm, tm), 0)
    col = lax.broadcasted_iota(jnp.int32, (tm, tm), 1)
    tri = jnp.where(row >= col, 1.0, 0.0).astype(jnp.bfloat16)
    hi, mid, lo = _split3(log_f)
    cs = _dot(tri, jnp.concatenate([hi, mid, lo], axis=1).astype(jnp.bfloat16))
    local = cs[:, :LANES] + cs[:, LANES:2 * LANES] + cs[:, 2 * LANES:]

    @pl.when(i == 0)
    def _():
        carry_ref[...] = jnp.zeros_like(carry_ref)

    f_cum = local + carry_ref[0:1, :]
    carry_ref[0:1, :] = f_cum[tm - 1:tm, :]
    f_log2 = f_cum * LOG2E
    f_cols = _split3(f_log2)
    f_rows = _split3(f_log2.T)

    qvt = _nt_dot(wqvt_ref[...], xn)
    _write_vt(vt_ref, qvt[SELF_WIDTH:])
    qm = _dot(xn, w_ref[:, FOX_QM_OFF:FOX_QM_OFF + MEM_WIDTH])
    _memory_attention(qm, mk_ref, mv_ref, hm_ref, lane)
    k = _dot(xn, w_ref[:, K_OFF:K_OFF + SELF_WIDTH])
    brow = lax.broadcasted_iota(jnp.int32, (SEL_ROW, tm), 0)
    tail = jnp.zeros((HEAD_DIM - SEL_ROW, tm), jnp.float32)
    for p in range(N_PAIRS):
        kp = k[:, LANES * p:LANES * (p + 1)]
        for e in range(2):
            hd = 2 * p + e
            base = HEAD_DIM * (1 - e)
            bias = jnp.where((brow >= 3) & (brow < 6), 1.0, 0.0)
            ext_k = jnp.where((lane >= base) & (lane < base + 3), 1.0, 0.0)
            for c in range(3):
                bias = jnp.where(brow == c, f_rows[c][hd:hd + 1], bias)
                ext_k = jnp.where(lane == base + 3 + c, -f_cols[c][:, hd:hd + 1], ext_k)
            data = qvt[HEAD_DIM * hd:HEAD_DIM * (hd + 1)] * (SCALE * LOG2E)
            parts = [data, bias, tail] if e == 0 else [bias, tail, data]
            qt_ref[0, LANES * hd:LANES * (hd + 1), :] = (
                jnp.concatenate(parts, axis=0).astype(qt_ref.dtype))
            in_head = (lane >= HEAD_DIM * e) & (lane < HEAD_DIM * (e + 1))
            ka_ref[:, LANES * hd:LANES * (hd + 1)] = (
                jnp.where(in_head, kp, ext_k).astype(ka_ref.dtype))


def _moba_proj_kernel(h_ref, g_ref, w_ref, wqvt_ref, mk_ref, mv_ref,
                      qt_ref, ka_ref, vt_ref, hm_ref, km_ref, *, slopes):
    i = pl.program_id(1)
    tm = h_ref.shape[0]
    xn = _rms(h_ref[...], g_ref[...]).astype(jnp.bfloat16)
    lane = lax.broadcasted_iota(jnp.int32, (tm, LANES), 1)
    lane_row = lax.broadcasted_iota(jnp.int32, (1, LANES), 1)

    qvt = _nt_dot(wqvt_ref[...], xn)
    _write_vt(vt_ref, qvt[SELF_WIDTH:])
    qm = _dot(xn, w_ref[:, MOBA_QM_OFF:MOBA_QM_OFF + MEM_WIDTH])
    _memory_attention(qm, mk_ref, mv_ref, hm_ref, lane)

    @pl.when(i == 0)
    def _():
        km_ref[...] = jnp.zeros_like(km_ref)

    k = _dot(xn, w_ref[:, K_OFF:K_OFF + SELF_WIDTH])
    kmean = jnp.mean(k, axis=0, keepdims=True)

    block_start = (i * tm).astype(jnp.float32)
    offset = lax.broadcasted_iota(jnp.int32, (tm, LANES), 0).astype(jnp.float32)
    ext_k = []
    for e in range(2):
        base = HEAD_DIM * (1 - e)
        x = jnp.where((lane >= base) & (lane < base + 3), 1.0, 0.0)
        x = jnp.where((lane >= base + 4) & (lane < base + 7), block_start, x)
        x = jnp.where((lane >= base + 7) & (lane < base + 10), offset, x)
        ext_k.append(jnp.where(lane == base + SEL_ROW + i, 1.0, x))

    pos_row = (i * tm + lax.broadcasted_iota(jnp.int32, (1, tm), 1)).astype(jnp.float32)
    blk = lax.broadcasted_iota(jnp.int32, (MAX_BLOCKS, tm), 0)
    brow = lax.broadcasted_iota(jnp.int32, (SEL_ROW, tm), 0)
    tail = jnp.zeros((HEAD_DIM - SEL_ROW - MAX_BLOCKS, tm), jnp.float32)

    for p in range(N_PAIRS):
        qt_pair = qvt[LANES * p:LANES * (p + 1)]
        kp = k[:, LANES * p:LANES * (p + 1)]
        q_hi = qt_pair.astype(jnp.bfloat16)
        q_lo = (qt_pair - q_hi.astype(jnp.float32)).astype(jnp.bfloat16)
        km = km_ref[p]
        km_hi = km.astype(jnp.bfloat16)
        km_lo = (km - km_hi.astype(jnp.float32)).astype(jnp.bfloat16)
        r = _dot(jnp.concatenate([km_hi, km_lo], axis=0), q_hi)
        gates = r[:2 * MAX_BLOCKS] + r[2 * MAX_BLOCKS:] + _dot(km_hi, q_lo)
        for e in range(2):
            hd = 2 * p + e
            g = jnp.where(blk < i, gates[MAX_BLOCKS * e:MAX_BLOCKS * (e + 1)], BELOW_NEG)
            chosen = blk == i
            for _ in range(MOBA_TOPK):
                best = jnp.max(g, axis=0, keepdims=True)
                first = jnp.min(jnp.where(g == best, blk, 2 * MAX_BLOCKS),
                                axis=0, keepdims=True)
                hit = blk == first
                chosen = chosen | (hit & (best > 0.5 * BELOW_NEG))
                g = jnp.where(hit, BELOW_NEG, g)
            pen = jnp.where(chosen, 0.0, NEG)

            slope = slopes[hd] * LOG2E
            t_terms = _split3(-slope * pos_row)
            s_terms = _split3(jnp.full((1, tm), slope, jnp.float32))
            bias = jnp.where(brow == 3, 1.0, 0.0)
            for c in range(3):
                bias = jnp.where(brow == c, t_terms[c], bias)
                bias = jnp.where((brow == 4 + c) | (brow == 7 + c), s_terms[c], bias)
            ext_q = jnp.concatenate([bias, pen, tail], axis=0)
            data = qt_pair[HEAD_DIM * e:HEAD_DIM * (e + 1)] * (SCALE * LOG2E)
            parts = [data, ext_q] if e == 0 else [ext_q, data]
            qt_ref[0, LANES * hd:LANES * (hd + 1), :] = (
                jnp.concatenate(parts, axis=0).astype(qt_ref.dtype))

            in_head = (lane >= HEAD_DIM * e) & (lane < HEAD_DIM * (e + 1))
            ka_ref[:, LANES * hd:LANES * (hd + 1)] = (
                jnp.where(in_head, kp, ext_k[e]).astype(ka_ref.dtype))

            in_head_row = (lane_row >= HEAD_DIM * e) & (lane_row < HEAD_DIM * (e + 1))
            km_row = jnp.where(in_head_row, kmean[:, LANES * p:LANES * (p + 1)], 0.0)
            km_ref[p, pl.ds(MAX_BLOCKS * e + i, 1), :] = km_row


def _flash_kernel(qt_ref, ka_ref, vt_ref, kpad_ref, o_ref,
                  kbuf, m_ref, acc_ref, s0, s1, p0, p1, a0, a1):
    n_sub = qt_ref.shape[0]
    tq = qt_ref.shape[2]
    tk = vt_ref.shape[2]
    nq = vt_ref.shape[0]
    seq = ka_ref.shape[0]

    @pl.when(pl.program_id(2) == 0)
    def _():
        kbuf[0:seq, :] = ka_ref[...]
        kbuf[seq:seq + tk, :] = kpad_ref[...]

    for sub in range(n_sub):
        _attend_query_tile(n_sub * pl.program_id(2) + sub, nq, qt_ref.at[sub],
                           o_ref.at[tq * sub:tq * (sub + 1)], vt_ref,
                           kbuf, m_ref, acc_ref, s0, s1, p0, p1, a0, a1)


def _attend_query_tile(i, nq, qt_ref, o_ref, vt_ref, kbuf, m_ref, acc_ref,
                       s0, s1, p0, p1, a0, a1):
    tq = qt_ref.shape[1]
    tk = vt_ref.shape[2]

    def scores(tile, s_ref):
        rows = pl.ds(pl.multiple_of(tile * tk, tk), tk)
        for e in range(2):
            st = _dot(kbuf[rows, LANES * e:LANES * (e + 1)],
                      qt_ref[LANES * e:LANES * (e + 1), :])
            s_ref[e, :, :tq] = st

    def softmax(s_ref, p_ref, a_ref, causal=False):
        for e in range(2):
            st = s_ref[e, :, :tq]
            if causal:
                kv = lax.broadcasted_iota(jnp.int32, (tk, tq), 0)
                qi = lax.broadcasted_iota(jnp.int32, (tk, tq), 1)
                st = jnp.where(kv <= qi, st, NEG)
            m_prev = m_ref[e]
            m_new = jnp.maximum(m_prev, jnp.max(st, axis=0, keepdims=True))
            a_ref[e] = jnp.exp2(m_prev - m_new)
            p_ref[e, :, :tq] = jnp.exp2(st - m_new).astype(p_ref.dtype)
            m_ref[e] = m_new

    def values(tile, p_ref, a_ref):
        for e in range(2):
            vt = vt_ref[tile, V_ROWS * e:V_ROWS * (e + 1), :]
            acc_ref[e] = a_ref[e] * acc_ref[e] + _dot(vt, p_ref[e, :, :tq])

    def key_tile(pos):
        return jnp.where(pos < i, pos, nq)

    def value_tile(pos):
        return jnp.where(pos < 0, i, jnp.minimum(pos, nq - 1))

    def pipeline_step(h):
        pos = 2 * h
        scores(key_tile(pos + 1), s1)
        values(value_tile(pos - 1), p1, a1)
        softmax(s0, p0, a0)
        scores(key_tile(pos + 2), s0)
        values(value_tile(pos), p0, a0)
        softmax(s1, p1, a1)

    m_ref[...] = jnp.full(m_ref.shape, NEG, jnp.float32)
    acc_ref[...] = jnp.zeros(acc_ref.shape, jnp.float32)
    scores(i, s1)
    scores(key_tile(0), s0)
    softmax(s1, p1, a1, causal=True)
    pipeline_step(0)

    done = 1
    for unroll in FLASH_UNROLLS:
        trips = (i // 2 + 1 - done) // unroll

        def body(t, carry, unroll=unroll, done=done):
            for u in range(unroll):
                pipeline_step(done + unroll * t + u)
            return carry

        lax.fori_loop(0, trips, body, 0)
        done = done + unroll * trips

    outs = []
    for e in range(2):
        acc = acc_ref[e]
        outs.append(acc[:HEAD_DIM] / acc[HEAD_DIM:HEAD_DIM + 1])
    o_ref[...] = jnp.concatenate(outs, axis=0).T.astype(o_ref.dtype)


def _ffn_kernel(hs_ref, hm_ref, h_ref, woa_ref, wob_ref, g_ref, wgu_ref, wd_ref,
                gf_ref, o_ref, *, final):
    h1 = h_ref[...] + _dot(hs_ref[...], woa_ref[...]) + _dot(hm_ref[...], wob_ref[...])
    hn = _rms(h1, g_ref[...]).astype(jnp.bfloat16)
    ffn = None
    for c0, c1 in ((0, 1536), (1536, D_FF)):
        gate = _dot(hn, wgu_ref[:, c0:c1])
        up = _dot(hn, wgu_ref[:, D_FF + c0:D_FF + c1])
        act = (gate * jax.nn.sigmoid(gate) * up).astype(jnp.bfloat16)
        down = _dot(act, wd_ref[c0:c1, :])
        ffn = down if ffn is None else ffn + down
    y = h1 + ffn
    if final:
        y = _rms(y, gf_ref[...])
    o_ref[...] = y


def _mem_kv_kernel(mem_ref, g_ref, w_ref, o_ref):
    mn = _rms(mem_ref[...], g_ref[0]).astype(jnp.bfloat16)
    o_ref[0] = _dot(mn, w_ref[0]).astype(o_ref.dtype)


def _params(sem):
    return pltpu.CompilerParams(dimension_semantics=sem, vmem_limit_bytes=VMEM_LIMIT)


def _const_spec(shape):
    nd = len(shape)
    return pl.BlockSpec(shape, lambda *_: (0,) * nd)


def _mem_kv(mem2, norm_mem, w_mem_kv, batch):
    depth = w_mem_kv.shape[0]
    return pl.pallas_call(
        _mem_kv_kernel,
        grid=(depth, batch),
        in_specs=[pl.BlockSpec((N_MEM, D_MODEL), lambda d, b: (b, 0)),
                  pl.BlockSpec((1, 1, D_MODEL), lambda d, b: (d, 0, 0)),
                  pl.BlockSpec((1, D_MODEL, 2 * MEM_WIDTH), lambda d, b: (d, 0, 0))],
        out_specs=pl.BlockSpec((1, N_MEM, 2 * MEM_WIDTH), lambda d, b: (d, b, 0)),
        out_shape=jax.ShapeDtypeStruct((depth, batch * N_MEM, 2 * MEM_WIDTH), jnp.bfloat16),
        compiler_params=_params(("arbitrary", "arbitrary")),
        name="mem_kv",
    )(mem2, norm_mem.reshape(depth, 1, D_MODEL), w_mem_kv)


def _proj_call(kernel_fn, name, h, gain, w, extra, mkv, batch, seq, scratch):
    wt = jnp.concatenate([w[:, Q_OFF:Q_OFF + SELF_WIDTH],
                          w[:, V_OFF:V_OFF + SELF_WIDTH]], axis=1).T
    nt = seq // TOKEN_TILE
    tokens = batch * seq
    tok = lambda b, i: (b * nt + i, 0)
    tile = lambda b, i: (b * nt + i, 0, 0)
    bf16 = jnp.bfloat16
    in_specs = [pl.BlockSpec((TOKEN_TILE, D_MODEL), tok),
                _const_spec((1, D_MODEL)),
                _const_spec(w.shape),
                _const_spec(wt.shape)]
    args = [h, gain, w, wt]
    for a in extra:
        in_specs.append(_const_spec(a.shape))
        args.append(a)
    in_specs += [pl.BlockSpec((N_MEM, MEM_WIDTH), lambda b, i: (b, 0)),
                 pl.BlockSpec((N_MEM, MEM_WIDTH), lambda b, i: (b, 1))]
    args += [mkv, mkv]
    out_specs = [pl.BlockSpec((1, AUG_WIDTH, TOKEN_TILE), tile),
                 pl.BlockSpec((TOKEN_TILE, AUG_WIDTH), tok),
                 pl.BlockSpec((1, N_SELF_HEADS * V_ROWS, TOKEN_TILE), tile),
                 pl.BlockSpec((TOKEN_TILE, MEM_WIDTH), tok)]
    out_shape = [jax.ShapeDtypeStruct((tokens // TOKEN_TILE, AUG_WIDTH, TOKEN_TILE), bf16),
                 jax.ShapeDtypeStruct((tokens, AUG_WIDTH), bf16),
                 jax.ShapeDtypeStruct((tokens // TOKEN_TILE, N_SELF_HEADS * V_ROWS, TOKEN_TILE),
                                      bf16),
                 jax.ShapeDtypeStruct((tokens, MEM_WIDTH), bf16)]
    return pl.pallas_call(
        kernel_fn,
        grid=(batch, nt),
        in_specs=in_specs,
        out_specs=out_specs,
        out_shape=out_shape,
        scratch_shapes=scratch,
        compiler_params=_params(("arbitrary", "arbitrary")),
        name=name,
    )(*args)


def _flash(qt, ka, vt, batch, seq):
    nq = seq // TOKEN_TILE
    tokens = batch * seq
    tq = TOKEN_TILE
    f32, bf16 = jnp.float32, jnp.bfloat16
    lane = jnp.arange(2 * LANES)
    ones_lane = (lane == HEAD_DIM + 3) | (lane == LANES + 3)
    kpad = jnp.broadcast_to(jnp.where(ones_lane, NEG, 0.0), (tq, 2 * LANES)).astype(bf16)
    n_sub = FLASH_QUERY_TILES
    steps = nq // n_sub
    return pl.pallas_call(
        _flash_kernel,
        grid=(batch, N_PAIRS, steps),
        in_specs=[pl.BlockSpec((n_sub, 2 * LANES, tq), lambda b, p, i: (b * steps + i, p, 0)),
                  pl.BlockSpec((seq, 2 * LANES), lambda b, p, i: (b, p)),
                  pl.BlockSpec((nq, 2 * V_ROWS, tq), lambda b, p, i: (b, p, 0)),
                  _const_spec(kpad.shape)],
        out_specs=pl.BlockSpec((n_sub * tq, LANES), lambda b, p, i: (b * steps + i, p)),
        out_shape=jax.ShapeDtypeStruct((tokens, SELF_WIDTH), bf16),
        scratch_shapes=[pltpu.VMEM((seq + tq, 2 * LANES), bf16),
                        pltpu.VMEM((2, 1, tq), f32), pltpu.VMEM((2, V_ROWS, tq), f32),
                        pltpu.VMEM((2, tq, tq + LANES), f32), pltpu.VMEM((2, tq, tq + LANES), f32),
                        pltpu.VMEM((2, tq, tq + LANES), bf16), pltpu.VMEM((2, tq, tq + LANES), bf16),
                        pltpu.VMEM((2, 1, tq), f32), pltpu.VMEM((2, 1, tq), f32)],
        compiler_params=_params(("arbitrary", "arbitrary", "arbitrary")),
        name="flash_attention",
    )(qt, ka, vt, kpad)


def _ffn(hs, hm, h, woa, wob, g, wgu, wd, gf, final):
    tokens = h.shape[0]
    tok = lambda t: (t, 0)
    return pl.pallas_call(
        functools.partial(_ffn_kernel, final=final),
        grid=(tokens // FFN_TILE,),
        in_specs=[pl.BlockSpec((FFN_TILE, SELF_WIDTH), tok),
                  pl.BlockSpec((FFN_TILE, MEM_WIDTH), tok),
                  pl.BlockSpec((FFN_TILE, D_MODEL), tok),
                  _const_spec(woa.shape), _const_spec(wob.shape),
                  _const_spec((1, D_MODEL)),
                  _const_spec(wgu.shape), _const_spec(wd.shape),
                  _const_spec((1, D_MODEL))],
        out_specs=pl.BlockSpec((FFN_TILE, D_MODEL), tok),
        out_shape=jax.ShapeDtypeStruct((tokens, D_MODEL), jnp.float32),
        compiler_params=_params(("arbitrary",)),
        name="out_proj_ffn",
    )(hs, hm, h, woa, wob, g, wgu, wd, gf)


def kernel(x, mem, norm_mix, norm_mem, norm_ffn, norm_final, w_in_fox, b_fgate,
           w_in_moba, w_mem_kv, w_out, w_gate_up, w_down):
    batch, seq, _ = x.shape
    depth = w_out.shape[0]
    bf16 = jnp.bfloat16
    assert seq % (TOKEN_TILE * FLASH_QUERY_TILES) == 0 and seq // MOBA_BLOCK <= MAX_BLOCKS
    assert (batch * seq) % FFN_TILE == 0

    slopes = tuple(2.0 ** (-8.0 * (hd + 1) / N_SELF_HEADS) for hd in range(N_SELF_HEADS))
    h = x.reshape(batch * seq, D_MODEL)
    mkv_all = _mem_kv(mem.reshape(batch * N_MEM, D_MODEL), norm_mem,
                      w_mem_kv.astype(bf16), batch)
    gf = norm_final.reshape(1, D_MODEL)

    for i in range(depth):
        j = i // 2
        gain = norm_mix[i].reshape(1, D_MODEL)
        if i % 2 == 0:
            w = w_in_fox[j]
            f_cols = jnp.pad(w[:, FOX_F_OFF:FOX_F_OFF + N_SELF_HEADS],
                             ((0, 0), (0, LANES - N_SELF_HEADS)))
            w = jnp.concatenate([w[:, :FOX_F_OFF], f_cols,
                                 w[:, FOX_F_OFF + N_SELF_HEADS:]], axis=1).astype(bf16)
            bias = jnp.pad(b_fgate[j], (0, LANES - N_SELF_HEADS)).reshape(1, LANES)
            qt, ka, vt, hm = _proj_call(
                _fox_proj_kernel, "fox_proj", h, gain, w, [bias], mkv_all[i], batch, seq,
                [pltpu.VMEM((8, LANES), jnp.float32)])
        else:
            qt, ka, vt, hm = _proj_call(
                functools.partial(_moba_proj_kernel, slopes=slopes), "moba_proj",
                h, gain, w_in_moba[j].astype(bf16), [], mkv_all[i], batch, seq,
                [pltpu.VMEM((N_PAIRS, 2 * MAX_BLOCKS, LANES), jnp.float32)])
        hs = _flash(qt, ka, vt, batch, seq)
        wo = w_out[i].astype(bf16)
        h = _ffn(hs, hm, h, wo[:SELF_WIDTH], wo[SELF_WIDTH:],
                 norm_ffn[i].reshape(1, D_MODEL), w_gate_up[i].astype(bf16),
                 w_down[i].astype(bf16), gf, final=(i == depth - 1))
    return h.reshape(batch, seq, D_MODEL)
```

```python
import functools

import jax
import jax.numpy as jnp
from jax import lax
from jax.experimental import pallas as pl
from jax.experimental.pallas import tpu as pltpu

D_MODEL = 1024
N_SELF_HEADS = 12
N_MEM_HEADS = 4
HEAD_DIM = 64
SELF_WIDTH = N_SELF_HEADS * HEAD_DIM
MEM_WIDTH = N_MEM_HEADS * HEAD_DIM
N_MEM = 256
D_FF = 2816
MOBA_BLOCK = 256
MOBA_TOPK = 3
RMS_EPS = 1e-6
NEG = -1e30
BELOW_NEG = -3e38

LANES = 128
N_PAIRS = N_SELF_HEADS // 2
AUG_WIDTH = N_SELF_HEADS * LANES
TOKEN_TILE = MOBA_BLOCK
FFN_TILE = 512
FLASH_UNROLLS = (8, 4, 2, 1)
FLASH_QUERY_TILES = 4
FOX_PROJ_TILES = 1
MOBA_PROJ_TILES = 4
MAX_BLOCKS = 32
SEL_ROW = 16
SCALE = HEAD_DIM ** -0.5
LOG2E = 1.4426950408889634
V_ROWS = 80
VMEM_LIMIT = 56 * 1024 * 1024

Q_OFF, K_OFF, V_OFF = 0, SELF_WIDTH, 2 * SELF_WIDTH
FOX_F_OFF = 3 * SELF_WIDTH
FOX_QM_OFF = FOX_F_OFF + LANES
MOBA_QM_OFF = 3 * SELF_WIDTH


def _nt_dot(a, b):
    return lax.dot_general(a, b, (((1,), (1,)), ((), ())),
                           preferred_element_type=jnp.float32)


def _dot(a, b):
    return jnp.dot(a, b, preferred_element_type=jnp.float32)


def _rms(xf, g):
    ms = jnp.mean(xf * xf, axis=-1, keepdims=True)
    return xf * lax.rsqrt(ms + RMS_EPS) * g


def _split3(x):
    hi = x.astype(jnp.bfloat16).astype(jnp.float32)
    r = x - hi
    mid = r.astype(jnp.bfloat16).astype(jnp.float32)
    return hi, mid, r - mid


def _memory_attention(qm, mk_ref, mv_ref, hm_ref, lane):
    for p in range(N_MEM_HEADS // 2):
        blk = qm[:, LANES * p:LANES * (p + 1)]
        mk = mk_ref[:, LANES * p:LANES * (p + 1)]
        mv = mv_ref[:, LANES * p:LANES * (p + 1)]
        outs = []
        for e in range(2):
            in_head = (lane >= HEAD_DIM * e) & (lane < HEAD_DIM * (e + 1))
            qh = jnp.where(in_head, blk, 0.0).astype(jnp.bfloat16)
            s = _nt_dot(qh, mk) * SCALE
            s = s - jnp.max(s, axis=-1, keepdims=True)
            pr = jnp.exp(s)
            den = jnp.sum(pr, axis=-1, keepdims=True)
            outs.append(_dot(pr.astype(jnp.bfloat16), mv) / den)
        out = jnp.where(lane < HEAD_DIM, outs[0], outs[1])
        hm_ref[:, LANES * p:LANES * (p + 1)] = out.astype(hm_ref.dtype)


def _write_vt(vt_ref, vt):
    tm = vt.shape[1]
    pad_rows = V_ROWS - HEAD_DIM
    ones_blk = jnp.where(lax.broadcasted_iota(jnp.int32, (pad_rows, tm), 0) == 0, 1.0, 0.0)
    for hd in range(N_SELF_HEADS):
        vt_ref[0, V_ROWS * hd:V_ROWS * hd + HEAD_DIM, :] = (
            vt[HEAD_DIM * hd:HEAD_DIM * (hd + 1)].astype(vt_ref.dtype))
        vt_ref[0, V_ROWS * hd + HEAD_DIM:V_ROWS * (hd + 1), :] = ones_blk.astype(vt_ref.dtype)


def _token_tile_views(sub, tm, h_ref, qt_ref, ka_ref, vt_ref, hm_ref):
    rows = slice(tm * sub, tm * (sub + 1))
    return (h_ref.at[rows], qt_ref.at[sub:sub + 1], ka_ref.at[rows],
            vt_ref.at[sub:sub + 1], hm_ref.at[rows])


def _fox_proj_kernel(h_ref, g_ref, w_ref, wqvt_ref, b_ref, mk_ref, mv_ref,
                     qt_ref, ka_ref, vt_ref, hm_ref, carry_ref):
    n_sub = qt_ref.shape[0]
    tm = h_ref.shape[0] // n_sub

    @pl.when(pl.program_id(1) == 0)
    def _():
        carry_ref[...] = jnp.zeros_like(carry_ref)

    for sub in range(n_sub):
        h_t, qt_t, ka_t, vt_t, hm_t = _token_tile_views(sub, tm, h_ref, qt_ref, ka_ref,
                                                        vt_ref, hm_ref)
        _fox_proj_tile(h_t, g_ref, w_ref, wqvt_ref, b_ref, mk_ref, mv_ref,
                       qt_t, ka_t, vt_t, hm_t, carry_ref)


def _fox_proj_tile(h_ref, g_ref, w_ref, wqvt_ref, b_ref, mk_ref, mv_ref,
                   qt_ref, ka_ref, vt_ref, hm_ref, carry_ref):
    tm = h_ref.shape[0]
    xn = _rms(h_ref[...], g_ref[...]).astype(jnp.bfloat16)
    lane = lax.broadcasted_iota(jnp.int32, (tm, LANES), 1)

    f_logit = _dot(xn, w_ref[:, FOX_F_OFF:FOX_F_OFF + LANES]) + b_ref[...]
    log_f = jnp.minimum(f_logit, 0.0) - jnp.log(1.0 + jnp.exp(-jnp.abs(f_logit)))
    row = lax.broadcasted_iota(jnp.int32, (tm, tm), 0)
    col = lax.broadcasted_iota(jnp.int32, (tm, tm), 1)
    tri = jnp.where(row >= col, 1.0, 0.0).astype(jnp.bfloat16)
    hi, mid, lo = _split3(log_f)
    cs = _dot(tri, jnp.concatenate([hi, mid, lo], axis=1).astype(jnp.bfloat16))
    local = cs[:, :LANES] + cs[:, LANES:2 * LANES] + cs[:, 2 * LANES:]
    f_cum = local + carry_ref[0:1, :]
    carry_ref[0:1, :] = f_cum[tm - 1:tm, :]
    f_log2 = f_cum * LOG2E
    f_cols = _split3(f_log2)
    f_rows = _split3(f_log2.T)

    qvt = _nt_dot(wqvt_ref[...], xn)
    _write_vt(vt_ref, qvt[SELF_WIDTH:])
    qm = _dot(xn, w_ref[:, FOX_QM_OFF:FOX_QM_OFF + MEM_WIDTH])
    _memory_attention(qm, mk_ref, mv_ref, hm_ref, lane)
    k = _dot(xn, w_ref[:, K_OFF:K_OFF + SELF_WIDTH])
    brow = lax.broadcasted_iota(jnp.int32, (SEL_ROW, tm), 0)
    tail = jnp.zeros((HEAD_DIM - SEL_ROW, tm), jnp.float32)
    for p in range(N_PAIRS):
        kp = k[:, LANES * p:LANES * (p + 1)]
        for e in range(2):
            hd = 2 * p + e
            base = HEAD_DIM * (1 - e)
            bias = jnp.where((brow >= 3) & (brow < 6), 1.0, 0.0)
            ext_k = jnp.where((lane >= base) & (lane < base + 3), 1.0, 0.0)
            for c in range(3):
                bias = jnp.where(brow == c, f_rows[c][hd:hd + 1], bias)
                ext_k = jnp.where(lane == base + 3 + c, -f_cols[c][:, hd:hd + 1], ext_k)
            data = qvt[HEAD_DIM * hd:HEAD_DIM * (hd + 1)] * (SCALE * LOG2E)
            parts = [data, bias, tail] if e == 0 else [bias, tail, data]
            qt_ref[0, LANES * hd:LANES * (hd + 1), :] = (
                jnp.concatenate(parts, axis=0).astype(qt_ref.dtype))
            in_head = (lane >= HEAD_DIM * e) & (lane < HEAD_DIM * (e + 1))
            ka_ref[:, LANES * hd:LANES * (hd + 1)] = (
                jnp.where(in_head, kp, ext_k).astype(ka_ref.dtype))


def _moba_proj_kernel(h_ref, g_ref, w_ref, wqvt_ref, mk_ref, mv_ref,
                      qt_ref, ka_ref, vt_ref, hm_ref, km_ref, *, slopes):
    n_sub = qt_ref.shape[0]
    tm = h_ref.shape[0] // n_sub

    @pl.when(pl.program_id(1) == 0)
    def _():
        km_ref[...] = jnp.zeros_like(km_ref)

    for sub in range(n_sub):
        h_t, qt_t, ka_t, vt_t, hm_t = _token_tile_views(sub, tm, h_ref, qt_ref, ka_ref,
                                                        vt_ref, hm_ref)
        _moba_proj_tile(n_sub * pl.program_id(1) + sub, h_t, g_ref, w_ref, wqvt_ref,
                        mk_ref, mv_ref, qt_t, ka_t, vt_t, hm_t, km_ref, slopes)


def _moba_proj_tile(i, h_ref, g_ref, w_ref, wqvt_ref, mk_ref, mv_ref,
                    qt_ref, ka_ref, vt_ref, hm_ref, km_ref, slopes):
    tm = h_ref.shape[0]
    xn = _rms(h_ref[...], g_ref[...]).astype(jnp.bfloat16)
    lane = lax.broadcasted_iota(jnp.int32, (tm, LANES), 1)
    lane_row = lax.broadcasted_iota(jnp.int32, (1, LANES), 1)

    qvt = _nt_dot(wqvt_ref[...], xn)
    _write_vt(vt_ref, qvt[SELF_WIDTH:])
    qm = _dot(xn, w_ref[:, MOBA_QM_OFF:MOBA_QM_OFF + MEM_WIDTH])
    _memory_attention(qm, mk_ref, mv_ref, hm_ref, lane)
    k = _dot(xn, w_ref[:, K_OFF:K_OFF + SELF_WIDTH])
    kmean = jnp.mean(k, axis=0, keepdims=True)

    block_start = (i * tm).astype(jnp.float32)
    offset = lax.broadcasted_iota(jnp.int32, (tm, LANES), 0).astype(jnp.float32)
    ext_k = []
    for e in range(2):
        base = HEAD_DIM * (1 - e)
        x = jnp.where((lane >= base) & (lane < base + 3), 1.0, 0.0)
        x = jnp.where((lane >= base + 4) & (lane < base + 7), block_start, x)
        x = jnp.where((lane >= base + 7) & (lane < base + 10), offset, x)
        ext_k.append(jnp.where(lane == base + SEL_ROW + i, 1.0, x))

    pos_row = (i * tm + lax.broadcasted_iota(jnp.int32, (1, tm), 1)).astype(jnp.float32)
    blk = lax.broadcasted_iota(jnp.int32, (MAX_BLOCKS, tm), 0)
    brow = lax.broadcasted_iota(jnp.int32, (SEL_ROW, tm), 0)
    tail = jnp.zeros((HEAD_DIM - SEL_ROW - MAX_BLOCKS, tm), jnp.float32)

    for p in range(N_PAIRS):
        qt_pair = qvt[LANES * p:LANES * (p + 1)]
        kp = k[:, LANES * p:LANES * (p + 1)]
        q_hi = qt_pair.astype(jnp.bfloat16)
        q_lo = (qt_pair - q_hi.astype(jnp.float32)).astype(jnp.bfloat16)
        km = km_ref[p]
        km_hi = km.astype(jnp.bfloat16)
        km_lo = (km - km_hi.astype(jnp.float32)).astype(jnp.bfloat16)
        r = _dot(jnp.concatenate([km_hi, km_lo], axis=0), q_hi)
        gates = r[:2 * MAX_BLOCKS] + r[2 * MAX_BLOCKS:] + _dot(km_hi, q_lo)
        for e in range(2):
            hd = 2 * p + e
            g = jnp.where(blk < i, gates[MAX_BLOCKS * e:MAX_BLOCKS * (e + 1)], BELOW_NEG)
            chosen = blk == i
            for _ in range(MOBA_TOPK):
                best = jnp.max(g, axis=0, keepdims=True)
                first = jnp.min(jnp.where(g == best, blk, 2 * MAX_BLOCKS),
                                axis=0, keepdims=True)
                hit = blk == first
                chosen = chosen | (hit & (best > 0.5 * BELOW_NEG))
                g = jnp.where(hit, BELOW_NEG, g)
            pen = jnp.where(chosen, 0.0, NEG)

            slope = slopes[hd] * LOG2E
            t_terms = _split3(-slope * pos_row)
            s_terms = _split3(jnp.full((1, tm), slope, jnp.float32))
            bias = jnp.where(brow == 3, 1.0, 0.0)
            for c in range(3):
                bias = jnp.where(brow == c, t_terms[c], bias)
                bias = jnp.where((brow == 4 + c) | (brow == 7 + c), s_terms[c], bias)
            ext_q = jnp.concatenate([bias, pen, tail], axis=0)
            data = qt_pair[HEAD_DIM * e:HEAD_DIM * (e + 1)] * (SCALE * LOG2E)
            parts = [data, ext_q] if e == 0 else [ext_q, data]
            qt_ref[0, LANES * hd:LANES * (hd + 1), :] = (
                jnp.concatenate(parts, axis=0).astype(qt_ref.dtype))

            in_head = (lane >= HEAD_DIM * e) & (lane < HEAD_DIM * (e + 1))
            ka_ref[:, LANES * hd:LANES * (hd + 1)] = (
                jnp.where(in_head, kp, ext_k[e]).astype(ka_ref.dtype))

            in_head_row = (lane_row >= HEAD_DIM * e) & (lane_row < HEAD_DIM * (e + 1))
            km_row = jnp.where(in_head_row, kmean[:, LANES * p:LANES * (p + 1)], 0.0)
            km_ref[p, pl.ds(MAX_BLOCKS * e + i, 1), :] = km_row


def _flash_kernel(qt_ref, ka_ref, vt_ref, kpad_ref, o_ref,
                  kbuf, m_ref, acc_ref, s0, s1, p0, p1, a0, a1):
    n_sub = qt_ref.shape[0]
    tq = qt_ref.shape[2]
    tk = vt_ref.shape[2]
    nq = vt_ref.shape[0]
    seq = ka_ref.shape[0]

    @pl.when(pl.program_id(2) == 0)
    def _():
        kbuf[0:seq, :] = ka_ref[...]
        kbuf[seq:seq + tk, :] = kpad_ref[...]

    for sub in range(n_sub):
        _attend_query_tile(n_sub * pl.program_id(2) + sub, nq, qt_ref.at[sub],
                           o_ref.at[tq * sub:tq * (sub + 1)], vt_ref,
                           kbuf, m_ref, acc_ref, s0, s1, p0, p1, a0, a1)


def _attend_query_tile(i, nq, qt_ref, o_ref, vt_ref, kbuf, m_ref, acc_ref,
                       s0, s1, p0, p1, a0, a1):
    tq = qt_ref.shape[1]
    tk = vt_ref.shape[2]

    def scores(tile, s_ref):
        rows = pl.ds(pl.multiple_of(tile * tk, tk), tk)
        for e in range(2):
            s_ref[e] = _dot(kbuf[rows, LANES * e:LANES * (e + 1)],
                            qt_ref[LANES * e:LANES * (e + 1), :])

    def softmax(s_ref, p_ref, a_ref, causal=False):
        for e in range(2):
            st = s_ref[e]
            if causal:
                kv = lax.broadcasted_iota(jnp.int32, (tk, tq), 0)
                qi = lax.broadcasted_iota(jnp.int32, (tk, tq), 1)
                st = jnp.where(kv <= qi, st, NEG)
            m_prev = m_ref[e]
            m_new = jnp.maximum(m_prev, jnp.max(st, axis=0, keepdims=True))
            a_ref[e] = jnp.exp2(m_prev - m_new)
            p_ref[e] = jnp.exp2(st - m_new).astype(p_ref.dtype)
            m_ref[e] = m_new

    def values(tile, p_ref, a_ref):
        for e in range(2):
            vt = vt_ref[tile, V_ROWS * e:V_ROWS * (e + 1), :]
            acc_ref[e] = a_ref[e] * acc_ref[e] + _dot(vt, p_ref[e])

    def key_tile(pos):
        return jnp.where(pos < i, pos, nq)

    def value_tile(pos):
        return jnp.where(pos < 0, i, jnp.minimum(pos, nq - 1))

    def pipeline_step(h):
        pos = 2 * h
        scores(key_tile(pos + 1), s1)
        values(value_tile(pos - 1), p1, a1)
        softmax(s0, p0, a0)
        scores(key_tile(pos + 2), s0)
        values(value_tile(pos), p0, a0)
        softmax(s1, p1, a1)

    m_ref[...] = jnp.full(m_ref.shape, NEG, jnp.float32)
    acc_ref[...] = jnp.zeros(acc_ref.shape, jnp.float32)
    scores(i, s1)
    scores(key_tile(0), s0)
    softmax(s1, p1, a1, causal=True)
    pipeline_step(0)

    done = 1
    for unroll in FLASH_UNROLLS:
        trips = (i // 2 + 1 - done) // unroll

        def body(t, carry, unroll=unroll, done=done):
            for u in range(unroll):
                pipeline_step(done + unroll * t + u)
            return carry

        lax.fori_loop(0, trips, body, 0)
        done = done + unroll * trips

    outs = []
    for e in range(2):
        acc = acc_ref[e]
        outs.append(acc[:HEAD_DIM] / acc[HEAD_DIM:HEAD_DIM + 1])
    o_ref[...] = jnp.concatenate(outs, axis=0).T.astype(o_ref.dtype)


def _ffn_kernel(hs_ref, hm_ref, h_ref, woa_ref, wob_ref, g_ref, wgu_ref, wd_ref,
                gf_ref, o_ref, *, final):
    h1 = h_ref[...] + _dot(hs_ref[...], woa_ref[...]) + _dot(hm_ref[...], wob_ref[...])
    hn = _rms(h1, g_ref[...]).astype(jnp.bfloat16)
    ffn = None
    for c0, c1 in ((0, 1536), (1536, D_FF)):
        gate = _dot(hn, wgu_ref[:, c0:c1])
        up = _dot(hn, wgu_ref[:, D_FF + c0:D_FF + c1])
        act = (gate * jax.nn.sigmoid(gate) * up).astype(jnp.bfloat16)
        down = _dot(act, wd_ref[c0:c1, :])
        ffn = down if ffn is None else ffn + down
    y = h1 + ffn
    if final:
        y = _rms(y, gf_ref[...])
    o_ref[...] = y


def _mem_kv_kernel(mem_ref, g_ref, w_ref, o_ref):
    mn = _rms(mem_ref[...], g_ref[0]).astype(jnp.bfloat16)
    o_ref[0] = _dot(mn, w_ref[0]).astype(o_ref.dtype)


def _params(sem):
    return pltpu.CompilerParams(dimension_semantics=sem, vmem_limit_bytes=VMEM_LIMIT)


def _const_spec(shape):
    nd = len(shape)
    return pl.BlockSpec(shape, lambda *_: (0,) * nd)


def _mem_kv(mem2, norm_mem, w_mem_kv, batch):
    depth = w_mem_kv.shape[0]
    return pl.pallas_call(
        _mem_kv_kernel,
        grid=(depth, batch),
        in_specs=[pl.BlockSpec((N_MEM, D_MODEL), lambda d, b: (b, 0)),
                  pl.BlockSpec((1, 1, D_MODEL), lambda d, b: (d, 0, 0)),
                  pl.BlockSpec((1, D_MODEL, 2 * MEM_WIDTH), lambda d, b: (d, 0, 0))],
        out_specs=pl.BlockSpec((1, N_MEM, 2 * MEM_WIDTH), lambda d, b: (d, b, 0)),
        out_shape=jax.ShapeDtypeStruct((depth, batch * N_MEM, 2 * MEM_WIDTH), jnp.bfloat16),
        compiler_params=_params(("arbitrary", "arbitrary")),
        name="mem_kv",
    )(mem2, norm_mem.reshape(depth, 1, D_MODEL), w_mem_kv)


def _proj_call(kernel_fn, name, h, gain, w, extra, mkv, batch, seq, scratch, n_sub):
    wt = jnp.concatenate([w[:, Q_OFF:Q_OFF + SELF_WIDTH],
                          w[:, V_OFF:V_OFF + SELF_WIDTH]], axis=1).T
    step_rows = n_sub * TOKEN_TILE
    nt = seq // step_rows
    tokens = batch * seq
    tok = lambda b, i: (b * nt + i, 0)
    tile = lambda b, i: (b * nt + i, 0, 0)
    bf16 = jnp.bfloat16
    in_specs = [pl.BlockSpec((step_rows, D_MODEL), tok),
                _const_spec((1, D_MODEL)),
                _const_spec(w.shape),
                _const_spec(wt.shape)]
    args = [h, gain, w, wt]
    for a in extra:
        in_specs.append(_const_spec(a.shape))
        args.append(a)
    in_specs += [pl.BlockSpec((N_MEM, MEM_WIDTH), lambda b, i: (b, 0)),
                 pl.BlockSpec((N_MEM, MEM_WIDTH), lambda b, i: (b, 1))]
    args += [mkv, mkv]
    out_specs = [pl.BlockSpec((n_sub, AUG_WIDTH, TOKEN_TILE), tile),
                 pl.BlockSpec((step_rows, AUG_WIDTH), tok),
                 pl.BlockSpec((n_sub, N_SELF_HEADS * V_ROWS, TOKEN_TILE), tile),
                 pl.BlockSpec((step_rows, MEM_WIDTH), tok)]
    out_shape = [jax.ShapeDtypeStruct((tokens // TOKEN_TILE, AUG_WIDTH, TOKEN_TILE), bf16),
                 jax.ShapeDtypeStruct((tokens, AUG_WIDTH), bf16),
                 jax.ShapeDtypeStruct((tokens // TOKEN_TILE, N_SELF_HEADS * V_ROWS, TOKEN_TILE),
                                      bf16),
                 jax.ShapeDtypeStruct((tokens, MEM_WIDTH), bf16)]
    return pl.pallas_call(
        kernel_fn,
        grid=(batch, nt),
        in_specs=in_specs,
        out_specs=out_specs,
        out_shape=out_shape,
        scratch_shapes=scratch,
        compiler_params=_params(("arbitrary", "arbitrary")),
        name=name,
    )(*args)


def _flash(qt, ka, vt, batch, seq):
    nq = seq // TOKEN_TILE
    tokens = batch * seq
    tq = TOKEN_TILE
    f32, bf16 = jnp.float32, jnp.bfloat16
    lane = jnp.arange(2 * LANES)
    ones_lane = (lane == HEAD_DIM + 3) | (lane == LANES + 3)
    kpad = jnp.broadcast_to(jnp.where(ones_lane, NEG, 0.0), (tq, 2 * LANES)).astype(bf16)
    n_sub = FLASH_QUERY_TILES
    steps = nq // n_sub
    return pl.pallas_call(
        _flash_kernel,
        grid=(batch, N_PAIRS, steps),
        in_specs=[pl.BlockSpec((n_sub, 2 * LANES, tq), lambda b, p, i: (b * steps + i, p, 0)),
                  pl.BlockSpec((seq, 2 * LANES), lambda b, p, i: (b, p)),
                  pl.BlockSpec((nq, 2 * V_ROWS, tq), lambda b, p, i: (b, p, 0)),
                  _const_spec(kpad.shape)],
        out_specs=pl.BlockSpec((n_sub * tq, LANES), lambda b, p, i: (b * steps + i, p)),
        out_shape=jax.ShapeDtypeStruct((tokens, SELF_WIDTH), bf16),
        scratch_shapes=[pltpu.VMEM((seq + tq, 2 * LANES), bf16),
                        pltpu.VMEM((2, 1, tq), f32), pltpu.VMEM((2, V_ROWS, tq), f32),
                        pltpu.VMEM((2, tq, tq), f32), pltpu.VMEM((2, tq, tq), f32),
                        pltpu.VMEM((2, tq, tq), bf16), pltpu.VMEM((2, tq, tq), bf16),
                        pltpu.VMEM((2, 1, tq), f32), pltpu.VMEM((2, 1, tq), f32)],
        compiler_params=_params(("arbitrary", "arbitrary", "arbitrary")),
        name="flash_attention",
    )(qt, ka, vt, kpad)


def _ffn(hs, hm, h, woa, wob, g, wgu, wd, gf, final):
    tokens = h.shape[0]
    tok = lambda t: (t, 0)
    return pl.pallas_call(
        functools.partial(_ffn_kernel, final=final),
        grid=(tokens // FFN_TILE,),
        in_specs=[pl.BlockSpec((FFN_TILE, SELF_WIDTH), tok),
                  pl.BlockSpec((FFN_TILE, MEM_WIDTH), tok),
                  pl.BlockSpec((FFN_TILE, D_MODEL), tok),
                  _const_spec(woa.shape), _const_spec(wob.shape),
                  _const_spec((1, D_MODEL)),
                  _const_spec(wgu.shape), _const_spec(wd.shape),
                  _const_spec((1, D_MODEL))],
        out_specs=pl.BlockSpec((FFN_TILE, D_MODEL), tok),
        out_shape=jax.ShapeDtypeStruct((tokens, D_MODEL), jnp.float32),
        compiler_params=_params(("arbitrary",)),
        name="out_proj_ffn",
    )(hs, hm, h, woa, wob, g, wgu, wd, gf)


def kernel(x, mem, norm_mix, norm_mem, norm_ffn, norm_final, w_in_fox, b_fgate,
           w_in_moba, w_mem_kv, w_out, w_gate_up, w_down):
    batch, seq, _ = x.shape
    depth = w_out.shape[0]
    bf16 = jnp.bfloat16
    assert seq % (TOKEN_TILE * FLASH_QUERY_TILES) == 0 and seq // MOBA_BLOCK <= MAX_BLOCKS
    assert seq % (TOKEN_TILE * max(FOX_PROJ_TILES, MOBA_PROJ_TILES)) == 0
    assert (batch * seq) % FFN_TILE == 0

    slopes = tuple(2.0 ** (-8.0 * (hd + 1) / N_SELF_HEADS) for hd in range(N_SELF_HEADS))
    h = x.reshape(batch * seq, D_MODEL)
    mkv_all = _mem_kv(mem.reshape(batch * N_MEM, D_MODEL), norm_mem,
                      w_mem_kv.astype(bf16), batch)
    gf = norm_final.reshape(1, D_MODEL)

    for i in range(depth):
        j = i // 2
        gain = norm_mix[i].reshape(1, D_MODEL)
        if i % 2 == 0:
            w = w_in_fox[j]
            f_cols = jnp.pad(w[:, FOX_F_OFF:FOX_F_OFF + N_SELF_HEADS],
                             ((0, 0), (0, LANES - N_SELF_HEADS)))
            w = jnp.concatenate([w[:, :FOX_F_OFF], f_cols,
                                 w[:, FOX_F_OFF + N_SELF_HEADS:]], axis=1).astype(bf16)
            bias = jnp.pad(b_fgate[j], (0, LANES - N_SELF_HEADS)).reshape(1, LANES)
            qt, ka, vt, hm = _proj_call(
                _fox_proj_kernel, "fox_proj", h, gain, w, [bias], mkv_all[i], batch, seq,
                [pltpu.VMEM((8, LANES), jnp.float32)], FOX_PROJ_TILES)
        else:
            qt, ka, vt, hm = _proj_call(
                functools.partial(_moba_proj_kernel, slopes=slopes), "moba_proj",
                h, gain, w_in_moba[j].astype(bf16), [], mkv_all[i], batch, seq,
                [pltpu.VMEM((N_PAIRS, 2 * MAX_BLOCKS, LANES), jnp.float32)], MOBA_PROJ_TILES)
        hs = _flash(qt, ka, vt, batch, seq)
        wo = w_out[i].astype(bf16)
        h = _ffn(hs, hm, h, wo[:SELF_WIDTH], wo[SELF_WIDTH:],
                 norm_ffn[i].reshape(1, D_MODEL), w_gate_up[i].astype(bf16),
                 w_down[i].astype(bf16), gf, final=(i == depth - 1))
    return h.reshape(batch, seq, D_MODEL)
```

```python
import functools

import jax
import jax.numpy as jnp
from jax import lax
from jax.experimental import pallas as pl
from jax.experimental.pallas import tpu as pltpu

D_MODEL = 1024
N_SELF_HEADS = 12
N_MEM_HEADS = 4
HEAD_DIM = 64
SELF_WIDTH = N_SELF_HEADS * HEAD_DIM
MEM_WIDTH = N_MEM_HEADS * HEAD_DIM
N_MEM = 256
D_FF = 2816
MOBA_BLOCK = 256
MOBA_TOPK = 3
RMS_EPS = 1e-6
NEG = -1e30
BELOW_NEG = -3e38

LANES = 128
N_PAIRS = N_SELF_HEADS // 2
AUG_WIDTH = N_SELF_HEADS * LANES
TOKEN_TILE = MOBA_BLOCK
FFN_TILE = 512
FLASH_UNROLLS = (8, 4, 2, 1)
FLASH_QUERY_TILES = 4
FOX_PROJ_TILES = 4
MOBA_PROJ_TILES = 4
MAX_BLOCKS = 32
SEL_ROW = 16
SCALE = HEAD_DIM ** -0.5
LOG2E = 1.4426950408889634
V_ROWS = 80
VMEM_LIMIT = 56 * 1024 * 1024

Q_OFF, K_OFF, V_OFF = 0, SELF_WIDTH, 2 * SELF_WIDTH
FOX_F_OFF = 3 * SELF_WIDTH
FOX_QM_OFF = FOX_F_OFF + LANES
MOBA_QM_OFF = 3 * SELF_WIDTH


def _nt_dot(a, b):
    return lax.dot_general(a, b, (((1,), (1,)), ((), ())),
                           preferred_element_type=jnp.float32)


def _dot(a, b):
    return jnp.dot(a, b, preferred_element_type=jnp.float32)


def _rms(xf, g):
    ms = jnp.mean(xf * xf, axis=-1, keepdims=True)
    return xf * lax.rsqrt(ms + RMS_EPS) * g


def _split3(x):
    hi = x.astype(jnp.bfloat16).astype(jnp.float32)
    r = x - hi
    mid = r.astype(jnp.bfloat16).astype(jnp.float32)
    return hi, mid, r - mid


def _memory_attention(qm, mk_ref, mv_ref, hm_ref, lane):
    for p in range(N_MEM_HEADS // 2):
        blk = qm[:, LANES * p:LANES * (p + 1)]
        mk = mk_ref[:, LANES * p:LANES * (p + 1)]
        mv = mv_ref[:, LANES * p:LANES * (p + 1)]
        outs = []
        for e in range(2):
            in_head = (lane >= HEAD_DIM * e) & (lane < HEAD_DIM * (e + 1))
            qh = jnp.where(in_head, blk, 0.0).astype(jnp.bfloat16)
            s = _nt_dot(qh, mk) * SCALE
            s = s - jnp.max(s, axis=-1, keepdims=True)
            pr = jnp.exp(s)
            den = jnp.sum(pr, axis=-1, keepdims=True)
            outs.append(_dot(pr.astype(jnp.bfloat16), mv) / den)
        out = jnp.where(lane < HEAD_DIM, outs[0], outs[1])
        hm_ref[:, LANES * p:LANES * (p + 1)] = out.astype(hm_ref.dtype)


def _write_vt(vt_ref, vt):
    tm = vt.shape[1]
    pad_rows = V_ROWS - HEAD_DIM
    ones_blk = jnp.where(lax.broadcasted_iota(jnp.int32, (pad_rows, tm), 0) == 0, 1.0, 0.0)
    for hd in range(N_SELF_HEADS):
        vt_ref[0, V_ROWS * hd:V_ROWS * hd + HEAD_DIM, :] = (
            vt[HEAD_DIM * hd:HEAD_DIM * (hd + 1)].astype(vt_ref.dtype))
        vt_ref[0, V_ROWS * hd + HEAD_DIM:V_ROWS * (hd + 1), :] = ones_blk.astype(vt_ref.dtype)


def _token_tile_views(sub, tm, h_ref, qt_ref, ka_ref, vt_ref, hm_ref):
    rows = slice(tm * sub, tm * (sub + 1))
    return (h_ref.at[rows], qt_ref.at[sub:sub + 1], ka_ref.at[rows],
            vt_ref.at[sub:sub + 1], hm_ref.at[rows])


def _fox_proj_kernel(h_ref, g_ref, w_ref, wqvt_ref, b_ref, mk_ref, mv_ref,
                     qt_ref, ka_ref, vt_ref, hm_ref, carry_ref):
    n_sub = qt_ref.shape[0]
    tm = h_ref.shape[0] // n_sub

    @pl.when(pl.program_id(1) == 0)
    def _():
        carry_ref[...] = jnp.zeros_like(carry_ref)

    for sub in range(n_sub):
        h_t, qt_t, ka_t, vt_t, hm_t = _token_tile_views(sub, tm, h_ref, qt_ref, ka_ref,
                                                        vt_ref, hm_ref)
        _fox_proj_tile(h_t, g_ref, w_ref, wqvt_ref, b_ref, mk_ref, mv_ref,
                       qt_t, ka_t, vt_t, hm_t, carry_ref)


def _fox_proj_tile(h_ref, g_ref, w_ref, wqvt_ref, b_ref, mk_ref, mv_ref,
                   qt_ref, ka_ref, vt_ref, hm_ref, carry_ref):
    tm = h_ref.shape[0]
    xn = _rms(h_ref[...], g_ref[...]).astype(jnp.bfloat16)
    lane = lax.broadcasted_iota(jnp.int32, (tm, LANES), 1)

    qvt = _nt_dot(wqvt_ref[...], xn)
    _write_vt(vt_ref, qvt[SELF_WIDTH:])
    qm = _dot(xn, w_ref[:, FOX_QM_OFF:FOX_QM_OFF + MEM_WIDTH])
    _memory_attention(qm, mk_ref, mv_ref, hm_ref, lane)

    f_logit = _dot(xn, w_ref[:, FOX_F_OFF:FOX_F_OFF + LANES]) + b_ref[...]
    log_f = jnp.minimum(f_logit, 0.0) - jnp.log(1.0 + jnp.exp(-jnp.abs(f_logit)))
    row = lax.broadcasted_iota(jnp.int32, (tm, tm), 0)
    col = lax.broadcasted_iota(jnp.int32, (tm, tm), 1)
    tri = jnp.where(row >= col, 1.0, 0.0).astype(jnp.bfloat16)
    hi, mid, lo = _split3(log_f)
    cs = _dot(tri, jnp.concatenate([hi, mid, lo], axis=1).astype(jnp.bfloat16))
    local = cs[:, :LANES] + cs[:, LANES:2 * LANES] + cs[:, 2 * LANES:]
    f_cum = local + carry_ref[0:1, :]
    carry_ref[0:1, :] = f_cum[tm - 1:tm, :]
    f_log2 = f_cum * LOG2E
    f_cols = _split3(f_log2)
    f_rows = _split3(f_log2.T)

    k = _dot(xn, w_ref[:, K_OFF:K_OFF + SELF_WIDTH])
    brow = lax.broadcasted_iota(jnp.int32, (SEL_ROW, tm), 0)
    tail = jnp.zeros((HEAD_DIM - SEL_ROW, tm), jnp.float32)
    for p in range(N_PAIRS):
        kp = k[:, LANES * p:LANES * (p + 1)]
        for e in range(2):
            hd = 2 * p + e
            base = HEAD_DIM * (1 - e)
            bias = jnp.where((brow >= 3) & (brow < 6), 1.0, 0.0)
            ext_k = jnp.where((lane >= base) & (lane < base + 3), 1.0, 0.0)
            for c in range(3):
                bias = jnp.where(brow == c, f_rows[c][hd:hd + 1], bias)
                ext_k = jnp.where(lane == base + 3 + c, -f_cols[c][:, hd:hd + 1], ext_k)
            data = qvt[HEAD_DIM * hd:HEAD_DIM * (hd + 1)] * (SCALE * LOG2E)
            parts = [data, bias, tail] if e == 0 else [bias, tail, data]
            qt_ref[0, LANES * hd:LANES * (hd + 1), :] = (
                jnp.concatenate(parts, axis=0).astype(qt_ref.dtype))
            in_head = (lane >= HEAD_DIM * e) & (lane < HEAD_DIM * (e + 1))
            ka_ref[:, LANES * hd:LANES * (hd + 1)] = (
                jnp.where(in_head, kp, ext_k).astype(ka_ref.dtype))


def _moba_proj_kernel(h_ref, g_ref, w_ref, wqvt_ref, mk_ref, mv_ref,
                      qt_ref, ka_ref, vt_ref, hm_ref, km_ref, *, slopes):
    n_sub = qt_ref.shape[0]
    tm = h_ref.shape[0] // n_sub

    @pl.when(pl.program_id(1) == 0)
    def _():
        km_ref[...] = jnp.zeros_like(km_ref)

    for sub in range(n_sub):
        h_t, qt_t, ka_t, vt_t, hm_t = _token_tile_views(sub, tm, h_ref, qt_ref, ka_ref,
                                                        vt_ref, hm_ref)
        _moba_proj_tile(n_sub * pl.program_id(1) + sub, h_t, g_ref, w_ref, wqvt_ref,
                        mk_ref, mv_ref, qt_t, ka_t, vt_t, hm_t, km_ref, slopes)


def _moba_proj_tile(i, h_ref, g_ref, w_ref, wqvt_ref, mk_ref, mv_ref,
                    qt_ref, ka_ref, vt_ref, hm_ref, km_ref, slopes):
    tm = h_ref.shape[0]
    xn = _rms(h_ref[...], g_ref[...]).astype(jnp.bfloat16)
    lane = lax.broadcasted_iota(jnp.int32, (tm, LANES), 1)
    lane_row = lax.broadcasted_iota(jnp.int32, (1, LANES), 1)

    qvt = _nt_dot(wqvt_ref[...], xn)
    _write_vt(vt_ref, qvt[SELF_WIDTH:])
    k = _dot(xn, w_ref[:, K_OFF:K_OFF + SELF_WIDTH])
    kmean = jnp.mean(k, axis=0, keepdims=True)

    block_start = (i * tm).astype(jnp.float32)
    offset = lax.broadcasted_iota(jnp.int32, (tm, LANES), 0).astype(jnp.float32)
    ext_k = []
    for e in range(2):
        base = HEAD_DIM * (1 - e)
        x = jnp.where((lane >= base) & (lane < base + 3), 1.0, 0.0)
        x = jnp.where((lane >= base + 4) & (lane < base + 7), block_start, x)
        x = jnp.where((lane >= base + 7) & (lane < base + 10), offset, x)
        ext_k.append(jnp.where(lane == base + SEL_ROW + i, 1.0, x))

    pos_row = (i * tm + lax.broadcasted_iota(jnp.int32, (1, tm), 1)).astype(jnp.float32)
    blk = lax.broadcasted_iota(jnp.int32, (MAX_BLOCKS, tm), 0)
    brow = lax.broadcasted_iota(jnp.int32, (SEL_ROW, tm), 0)
    tail = jnp.zeros((HEAD_DIM - SEL_ROW - MAX_BLOCKS, tm), jnp.float32)

    for p in range(N_PAIRS):
        qt_pair = qvt[LANES * p:LANES * (p + 1)]
        kp = k[:, LANES * p:LANES * (p + 1)]
        q_hi = qt_pair.astype(jnp.bfloat16)
        q_lo = (qt_pair - q_hi.astype(jnp.float32)).astype(jnp.bfloat16)
        km = km_ref[p]
        km_hi = km.astype(jnp.bfloat16)
        km_lo = (km - km_hi.astype(jnp.float32)).astype(jnp.bfloat16)
        r = _dot(jnp.concatenate([km_hi, km_lo], axis=0), q_hi)
        gates = r[:2 * MAX_BLOCKS] + r[2 * MAX_BLOCKS:] + _dot(km_hi, q_lo)
        for e in range(2):
            hd = 2 * p + e
            g = jnp.where(blk < i, gates[MAX_BLOCKS * e:MAX_BLOCKS * (e + 1)], BELOW_NEG)
            chosen = blk == i
            for _ in range(MOBA_TOPK):
                best = jnp.max(g, axis=0, keepdims=True)
                first = jnp.min(jnp.where(g == best, blk, 2 * MAX_BLOCKS),
                                axis=0, keepdims=True)
                hit = blk == first
                chosen = chosen | (hit & (best > 0.5 * BELOW_NEG))
                g = jnp.where(hit, BELOW_NEG, g)
            pen = jnp.where(chosen, 0.0, NEG)

            slope = slopes[hd] * LOG2E
            t_terms = _split3(-slope * pos_row)
            s_terms = _split3(jnp.full((1, tm), slope, jnp.float32))
            bias = jnp.where(brow == 3, 1.0, 0.0)
            for c in range(3):
                bias = jnp.where(brow == c, t_terms[c], bias)
                bias = jnp.where((brow == 4 + c) | (brow == 7 + c), s_terms[c], bias)
            ext_q = jnp.concatenate([bias, pen, tail], axis=0)
            data = qt_pair[HEAD_DIM * e:HEAD_DIM * (e + 1)] * (SCALE * LOG2E)
            parts = [data, ext_q] if e == 0 else [ext_q, data]
            qt_ref[0, LANES * hd:LANES * (hd + 1), :] = (
                jnp.concatenate(parts, axis=0).astype(qt_ref.dtype))

            in_head = (lane >= HEAD_DIM * e) & (lane < HEAD_DIM * (e + 1))
            ka_ref[:, LANES * hd:LANES * (hd + 1)] = (
                jnp.where(in_head, kp, ext_k[e]).astype(ka_ref.dtype))

            in_head_row = (lane_row >= HEAD_DIM * e) & (lane_row < HEAD_DIM * (e + 1))
            km_row = jnp.where(in_head_row, kmean[:, LANES * p:LANES * (p + 1)], 0.0)
            km_ref[p, pl.ds(MAX_BLOCKS * e + i, 1), :] = km_row

    qm = _dot(xn, w_ref[:, MOBA_QM_OFF:MOBA_QM_OFF + MEM_WIDTH])
    _memory_attention(qm, mk_ref, mv_ref, hm_ref, lane)


def _flash_kernel(qt_ref, ka_ref, vt_ref, kpad_ref, o_ref,
                  kbuf, m_ref, acc_ref, s0, s1, p0, p1, a0, a1):
    n_sub = qt_ref.shape[0]
    tq = qt_ref.shape[2]
    tk = vt_ref.shape[2]
    nq = vt_ref.shape[0]
    seq = ka_ref.shape[0]

    @pl.when(pl.program_id(2) == 0)
    def _():
        kbuf[0:seq, :] = ka_ref[...]
        kbuf[seq:seq + tk, :] = kpad_ref[...]

    for sub in range(n_sub):
        _attend_query_tile(n_sub * pl.program_id(2) + sub, nq, qt_ref.at[sub],
                           o_ref.at[tq * sub:tq * (sub + 1)], vt_ref,
                           kbuf, m_ref, acc_ref, s0, s1, p0, p1, a0, a1)


def _attend_query_tile(i, nq, qt_ref, o_ref, vt_ref, kbuf, m_ref, acc_ref,
                       s0, s1, p0, p1, a0, a1):
    tq = qt_ref.shape[1]
    tk = vt_ref.shape[2]

    def scores(tile, s_ref):
        rows = pl.ds(pl.multiple_of(tile * tk, tk), tk)
        for e in range(2):
            s_ref[e] = _dot(kbuf[rows, LANES * e:LANES * (e + 1)],
                            qt_ref[LANES * e:LANES * (e + 1), :])

    def softmax(s_ref, p_ref, a_ref, causal=False):
        for e in range(2):
            st = s_ref[e]
            if causal:
                kv = lax.broadcasted_iota(jnp.int32, (tk, tq), 0)
                qi = lax.broadcasted_iota(jnp.int32, (tk, tq), 1)
                st = jnp.where(kv <= qi, st, NEG)
            m_prev = m_ref[e]
            m_new = jnp.maximum(m_prev, jnp.max(st, axis=0, keepdims=True))
            a_ref[e] = jnp.exp2(m_prev - m_new)
            p_ref[e] = jnp.exp2(st - m_new).astype(p_ref.dtype)
            m_ref[e] = m_new

    def values(tile, p_ref, a_ref):
        for e in range(2):
            vt = vt_ref[tile, V_ROWS * e:V_ROWS * (e + 1), :]
            acc_ref[e] = a_ref[e] * acc_ref[e] + _dot(vt, p_ref[e])

    def key_tile(pos):
        return jnp.where(pos < i, pos, nq)

    def value_tile(pos):
        return jnp.where(pos < 0, i, jnp.minimum(pos, nq - 1))

    def pipeline_step(h):
        pos = 2 * h
        scores(key_tile(pos + 1), s1)
        values(value_tile(pos - 1), p1, a1)
        softmax(s0, p0, a0)
        scores(key_tile(pos + 2), s0)
        values(value_tile(pos), p0, a0)
        softmax(s1, p1, a1)

    m_ref[...] = jnp.full(m_ref.shape, NEG, jnp.float32)
    acc_ref[...] = jnp.zeros(acc_ref.shape, jnp.float32)
    scores(i, s1)
    scores(key_tile(0), s0)
    softmax(s1, p1, a1, causal=True)
    pipeline_step(0)

    done = 1
    for unroll in FLASH_UNROLLS:
        trips = (i // 2 + 1 - done) // unroll

        def body(t, carry, unroll=unroll, done=done):
            for u in range(unroll):
                pipeline_step(done + unroll * t + u)
            return carry

        lax.fori_loop(0, trips, body, 0)
        done = done + unroll * trips

    outs = []
    for e in range(2):
        acc = acc_ref[e]
        outs.append(acc[:HEAD_DIM] / acc[HEAD_DIM:HEAD_DIM + 1])
    o_ref[...] = jnp.concatenate(outs, axis=0).T.astype(o_ref.dtype)


def _ffn_kernel(hs_ref, hm_ref, h_ref, woa_ref, wob_ref, g_ref, wgu_ref, wd_ref,
                gf_ref, o_ref, *, final):
    h1 = h_ref[...] + _dot(hs_ref[...], woa_ref[...]) + _dot(hm_ref[...], wob_ref[...])
    hn = _rms(h1, g_ref[...]).astype(jnp.bfloat16)
    ffn = None
    for c0, c1 in ((0, 1536), (1536, D_FF)):
        gate = _dot(hn, wgu_ref[:, c0:c1])
        up = _dot(hn, wgu_ref[:, D_FF + c0:D_FF + c1])
        act = (gate * jax.nn.sigmoid(gate) * up).astype(jnp.bfloat16)
        down = _dot(act, wd_ref[c0:c1, :])
        ffn = down if ffn is None else ffn + down
    y = h1 + ffn
    if final:
        y = _rms(y, gf_ref[...])
    o_ref[...] = y


def _mem_kv_kernel(mem_ref, g_ref, w_ref, o_ref):
    mn = _rms(mem_ref[...], g_ref[0]).astype(jnp.bfloat16)
    o_ref[0] = _dot(mn, w_ref[0]).astype(o_ref.dtype)


def _params(sem):
    return pltpu.CompilerParams(dimension_semantics=sem, vmem_limit_bytes=VMEM_LIMIT)


def _const_spec(shape):
    nd = len(shape)
    return pl.BlockSpec(shape, lambda *_: (0,) * nd)


def _mem_kv(mem2, norm_mem, w_mem_kv, batch):
    depth = w_mem_kv.shape[0]
    return pl.pallas_call(
        _mem_kv_kernel,
        grid=(depth, batch),
        in_specs=[pl.BlockSpec((N_MEM, D_MODEL), lambda d, b: (b, 0)),
                  pl.BlockSpec((1, 1, D_MODEL), lambda d, b: (d, 0, 0)),
                  pl.BlockSpec((1, D_MODEL, 2 * MEM_WIDTH), lambda d, b: (d, 0, 0))],
        out_specs=pl.BlockSpec((1, N_MEM, 2 * MEM_WIDTH), lambda d, b: (d, b, 0)),
        out_shape=jax.ShapeDtypeStruct((depth, batch * N_MEM, 2 * MEM_WIDTH), jnp.bfloat16),
        compiler_params=_params(("arbitrary", "arbitrary")),
        name="mem_kv",
    )(mem2, norm_mem.reshape(depth, 1, D_MODEL), w_mem_kv)


def _proj_call(kernel_fn, name, h, gain, w, extra, mkv, batch, seq, scratch, n_sub):
    wt = jnp.concatenate([w[:, Q_OFF:Q_OFF + SELF_WIDTH],
                          w[:, V_OFF:V_OFF + SELF_WIDTH]], axis=1).T
    step_rows = n_sub * TOKEN_TILE
    nt = seq // step_rows
    tokens = batch * seq
    tok = lambda b, i: (b * nt + i, 0)
    tile = lambda b, i: (b * nt + i, 0, 0)
    bf16 = jnp.bfloat16
    in_specs = [pl.BlockSpec((step_rows, D_MODEL), tok),
                _const_spec((1, D_MODEL)),
                _const_spec(w.shape),
                _const_spec(wt.shape)]
    args = [h, gain, w, wt]
    for a in extra:
        in_specs.append(_const_spec(a.shape))
        args.append(a)
    in_specs += [pl.BlockSpec((N_MEM, MEM_WIDTH), lambda b, i: (b, 0)),
                 pl.BlockSpec((N_MEM, MEM_WIDTH), lambda b, i: (b, 1))]
    args += [mkv, mkv]
    out_specs = [pl.BlockSpec((n_sub, AUG_WIDTH, TOKEN_TILE), tile),
                 pl.BlockSpec((step_rows, AUG_WIDTH), tok),
                 pl.BlockSpec((n_sub, N_SELF_HEADS * V_ROWS, TOKEN_TILE), tile),
                 pl.BlockSpec((step_rows, MEM_WIDTH), tok)]
    out_shape = [jax.ShapeDtypeStruct((tokens // TOKEN_TILE, AUG_WIDTH, TOKEN_TILE), bf16),
                 jax.ShapeDtypeStruct((tokens, AUG_WIDTH), bf16),
                 jax.ShapeDtypeStruct((tokens // TOKEN_TILE, N_SELF_HEADS * V_ROWS, TOKEN_TILE),
                                      bf16),
                 jax.ShapeDtypeStruct((tokens, MEM_WIDTH), bf16)]
    return pl.pallas_call(
        kernel_fn,
        grid=(batch, nt),
        in_specs=in_specs,
        out_specs=out_specs,
        out_shape=out_shape,
        scratch_shapes=scratch,
        compiler_params=_params(("arbitrary", "arbitrary")),
        name=name,
    )(*args)


def _flash(qt, ka, vt, batch, seq):
    nq = seq // TOKEN_TILE
    tokens = batch * seq
    tq = TOKEN_TILE
    f32, bf16 = jnp.float32, jnp.bfloat16
    lane = jnp.arange(2 * LANES)
    ones_lane = (lane == HEAD_DIM + 3) | (lane == LANES + 3)
    kpad = jnp.broadcast_to(jnp.where(ones_lane, NEG, 0.0), (tq, 2 * LANES)).astype(bf16)
    n_sub = FLASH_QUERY_TILES
    steps = nq // n_sub
    return pl.pallas_call(
        _flash_kernel,
        grid=(batch, N_PAIRS, steps),
        in_specs=[pl.BlockSpec((n_sub, 2 * LANES, tq), lambda b, p, i: (b * steps + i, p, 0)),
                  pl.BlockSpec((seq, 2 * LANES), lambda b, p, i: (b, p)),
                  pl.BlockSpec((nq, 2 * V_ROWS, tq), lambda b, p, i: (b, p, 0)),
                  _const_spec(kpad.shape)],
        out_specs=pl.BlockSpec((n_sub * tq, LANES), lambda b, p, i: (b * steps + i, p)),
        out_shape=jax.ShapeDtypeStruct((tokens, SELF_WIDTH), bf16),
        scratch_shapes=[pltpu.VMEM((seq + tq, 2 * LANES), bf16),
                        pltpu.VMEM((2, 1, tq), f32), pltpu.VMEM((2, V_ROWS, tq), f32),
                        pltpu.VMEM((2, tq, tq), f32), pltpu.VMEM((2, tq, tq), f32),
                        pltpu.VMEM((2, tq, tq), bf16), pltpu.VMEM((2, tq, tq), bf16),
                        pltpu.VMEM((2, 1, tq), f32), pltpu.VMEM((2, 1, tq), f32)],
        compiler_params=_params(("arbitrary", "arbitrary", "arbitrary")),
        name="flash_attention",
    )(qt, ka, vt, kpad)


def _ffn(hs, hm, h, woa, wob, g, wgu, wd, gf, final):
    tokens = h.shape[0]
    tok = lambda t: (t, 0)
    return pl.pallas_call(
        functools.partial(_ffn_kernel, final=final),
        grid=(tokens // FFN_TILE,),
        in_specs=[pl.BlockSpec((FFN_TILE, SELF_WIDTH), tok),
                  pl.BlockSpec((FFN_TILE, MEM_WIDTH), tok),
                  pl.BlockSpec((FFN_TILE, D_MODEL), tok),
                  _const_spec(woa.shape), _const_spec(wob.shape),
                  _const_spec((1, D_MODEL)),
                  _const_spec(wgu.shape), _const_spec(wd.shape),
                  _const_spec((1, D_MODEL))],
        out_specs=pl.BlockSpec((FFN_TILE, D_MODEL), tok),
        out_shape=jax.ShapeDtypeStruct((tokens, D_MODEL), jnp.float32),
        compiler_params=_params(("arbitrary",)),
        name="out_proj_ffn",
    )(hs, hm, h, woa, wob, g, wgu, wd, gf)


def kernel(x, mem, norm_mix, norm_mem, norm_ffn, norm_final, w_in_fox, b_fgate,
           w_in_moba, w_mem_kv, w_out, w_gate_up, w_down):
    batch, seq, _ = x.shape
    depth = w_out.shape[0]
    bf16 = jnp.bfloat16
    assert seq % (TOKEN_TILE * FLASH_QUERY_TILES) == 0 and seq // MOBA_BLOCK <= MAX_BLOCKS
    assert seq % (TOKEN_TILE * max(FOX_PROJ_TILES, MOBA_PROJ_TILES)) == 0
    assert (batch * seq) % FFN_TILE == 0

    slopes = tuple(2.0 ** (-8.0 * (hd + 1) / N_SELF_HEADS) for hd in range(N_SELF_HEADS))
    h = x.reshape(batch * seq, D_MODEL)
    mkv_all = _mem_kv(mem.reshape(batch * N_MEM, D_MODEL), norm_mem,
                      w_mem_kv.astype(bf16), batch)
    gf = norm_final.reshape(1, D_MODEL)

    for i in range(depth):
        j = i // 2
        gain = norm_mix[i].reshape(1, D_MODEL)
        if i % 2 == 0:
            w = w_in_fox[j]
            f_cols = jnp.pad(w[:, FOX_F_OFF:FOX_F_OFF + N_SELF_HEADS],
                             ((0, 0), (0, LANES - N_SELF_HEADS)))
            w = jnp.concatenate([w[:, :FOX_F_OFF], f_cols,
                                 w[:, FOX_F_OFF + N_SELF_HEADS:]], axis=1).astype(bf16)
            bias = jnp.pad(b_fgate[j], (0, LANES - N_SELF_HEADS)).reshape(1, LANES)
            qt, ka, vt, hm = _proj_call(
                _fox_proj_kernel, "fox_proj", h, gain, w, [bias], mkv_all[i], batch, seq,
                [pltpu.VMEM((8, LANES), jnp.float32)], FOX_PROJ_TILES)
        else:
            qt, ka, vt, hm = _proj_call(
                functools.partial(_moba_proj_kernel, slopes=slopes), "moba_proj",
                h, gain, w_in_moba[j].astype(bf16), [], mkv_all[i], batch, seq,
                [pltpu.VMEM((N_PAIRS, 2 * MAX_BLOCKS, LANES), jnp.float32)], MOBA_PROJ_TILES)
        hs = _flash(qt, ka, vt, batch, seq)
        wo = w_out[i].astype(bf16)
        h = _ffn(hs, hm, h, wo[:SELF_WIDTH], wo[SELF_WIDTH:],
                 norm_ffn[i].reshape(1, D_MODEL), w_gate_up[i].astype(bf16),
                 w_down[i].astype(bf16), gf, final=(i == depth - 1))
    return h.reshape(batch, seq, D_MODEL)
```
